```python
import jax, jax.numpy as jnp
from jax import lax
import numpy as np

D_MODEL = 2048
BATCH = 4
SEQ = 4096
DEPTH = 2

CHUNK = 64
NORM_EPS = 1e-6
A_HEADS = 16
A_HEAD_DIM = 64
A_WIDTH = A_HEADS * A_HEAD_DIM
LORA_W = 64
LORA_A = 64
LORA_G = 160
A_GN_EPS = 64e-5
A_IN = 3 * A_WIDTH + LORA_W + LORA_A + LORA_G
B_HEADS = 8
B_HEAD_DIM = 128
B_WIDTH = B_HEADS * B_HEAD_DIM
KV_LATENT = 256
IDX_HEADS = 16
IDX_HEAD_DIM = 64
TOPK_MAX = 256
Q_BLOCK = 128
B_IN = B_WIDTH + KV_LATENT + IDX_HEADS * IDX_HEAD_DIM + IDX_HEAD_DIM + IDX_HEADS
AB_IN = A_IN + B_IN
AB_OUT = A_WIDTH + B_WIDTH
SG_CHUNK = 128
SG_WIDTH = D_MODEL
SG_GROUPS = 8
SG_GROUP_DIM = SG_WIDTH // SG_GROUPS
D_FF = 5632
CONV_W = 3
N_EVEN = (DEPTH + 1) // 2
N_ODD = DEPTH // 2

kernel_name = 'hybrid_rwkv7_dsa_sgu_convffn'


def _rmsnorm(x, g, eps=NORM_EPS):
    xf = x.astype(jnp.float32)
    y = xf * lax.rsqrt(jnp.mean(xf * xf, axis=-1, keepdims=True) + eps)
    return (y * g.astype(jnp.float32)).astype(x.dtype)


def _layernorm(x, g, b, eps=1e-5):
    xf = x.astype(jnp.float32)
    mu = jnp.mean(xf, axis=-1, keepdims=True)
    var = jnp.mean(jnp.square(xf - mu), axis=-1, keepdims=True)
    return ((xf - mu) * lax.rsqrt(var + eps) * g + b).astype(x.dtype)


def _shift(z):
    return jnp.pad(z, ((0, 0), (1, 0), (0, 0)))[:, :z.shape[1]]


def _rwkv7_mixer(za, mu, w0, w_up, a0, a_up, g_up, k_k, k_a, r_k, gn_g, gn_b):
    Bn, T, _ = za.shape
    za = za + (_shift(za) - za) * mu
    r, k, v, xw, xa, xg = jnp.split(
        za, (A_WIDTH, 2 * A_WIDTH, 3 * A_WIDTH, 3 * A_WIDTH + LORA_W, 3 * A_WIDTH + LORA_W + LORA_A), axis=-1)
    log_w = -jax.nn.softplus(-(w0 + jnp.tanh(xw) @ w_up)) - 0.5
    decay = jnp.exp(-jnp.exp(log_w.astype(jnp.float32)))
    a = jax.nn.sigmoid((a0 + xa @ a_up).astype(jnp.float32))
    g = jax.nn.sigmoid(xg) @ g_up
    r = r.astype(jnp.float32)
    v = v.astype(jnp.float32)
    k = k.astype(jnp.float32)
    kk = (k * k_k).reshape(Bn, T, A_HEADS, A_HEAD_DIM)
    kk = kk * lax.rsqrt(jnp.sum(kk * kk, axis=-1, keepdims=True) + 1e-12)
    k = k * (1.0 + (a - 1.0) * k_a)
    heads = lambda t: t.reshape(Bn, T, A_HEADS, A_HEAD_DIM)
    r, k, v, decay, a = heads(r), heads(k), heads(v), heads(decay), heads(a)

    def step(S, inp):
        r_t, w_t, k_t, v_t, kk_t, a_t = inp
        sa = jnp.einsum('bhij,bhj->bhi', S, -kk_t)
        S = (S * w_t[:, :, None, :] + sa[..., None] * (kk_t * a_t)[:, :, None, :]
             + v_t[..., None] * k_t[:, :, None, :])
        return S, jnp.einsum('bhij,bhj->bhi', S, r_t)

    tm = lambda t: jnp.moveaxis(t, 1, 0)
    S0 = jnp.zeros((Bn, A_HEADS, A_HEAD_DIM, A_HEAD_DIM), jnp.float32)
    _, y = lax.scan(step, S0, (tm(r), tm(decay), tm(k), tm(v), tm(kk), tm(a)))
    y = jnp.moveaxis(y, 0, 1)
    y_mu = jnp.mean(y, axis=-1, keepdims=True)
    y_var = jnp.mean(jnp.square(y - y_mu), axis=-1, keepdims=True)
    y = ((y - y_mu) * lax.rsqrt(y_var + A_GN_EPS)).reshape(Bn, T, A_WIDTH) * gn_g + gn_b
    bonus = jnp.sum(r * k * r_k, axis=-1, keepdims=True) * v
    y = (y + bonus.reshape(Bn, T, A_WIDTH)) * g
    return y.astype(za.dtype)


def _dsa_mixer(zb, kv_norm, w_uk, w_uv, idx_k_norm):
    Bn, T, _ = zb.shape
    o1 = B_WIDTH
    o2 = o1 + KV_LATENT
    o3 = o2 + IDX_HEADS * IDX_HEAD_DIM
    o4 = o3 + IDX_HEAD_DIM
    q, ckv, q_idx, k_idx, w_idx = jnp.split(zb, (o1, o2, o3, o4), axis=-1)
    ckv = _rmsnorm(ckv, kv_norm).astype(jnp.float32)
    k_idx = _rmsnorm(k_idx, idx_k_norm).astype(jnp.float32)
    q_lat = jnp.einsum('bthd,hdc->bthc', q.reshape(Bn, T, B_HEADS, B_HEAD_DIM),
                       w_uk).astype(jnp.float32) * (B_HEAD_DIM ** -0.5)
    q_idx = q_idx.reshape(Bn, T, IDX_HEADS, IDX_HEAD_DIM).astype(jnp.float32) * (IDX_HEAD_DIM ** -0.5)
    w_idx = w_idx.astype(jnp.float32) * (IDX_HEADS ** -0.5)
    topk = min(TOPK_MAX, T // 4)
    nblk = T // Q_BLOCK
    to_blocks = lambda t: jnp.moveaxis(t.reshape(Bn, nblk, Q_BLOCK, *t.shape[2:]), 1, 0)
    key_pos = jnp.arange(T)

    def block(args):
        qi, wi, ql, blk = args
        qpos = blk * Q_BLOCK + jnp.arange(Q_BLOCK)
        limit = (qpos // CHUNK + 1) * CHUNK
        allowed = key_pos[None, :] < limit[:, None]
        rel = jax.nn.relu(jnp.einsum('bqhd,bsd->bqhs', qi, k_idx))
        score = jnp.einsum('bqh,bqhs->bqs', wi, rel)
        score = jnp.where(allowed[None], score, -jnp.inf)
        _, sel = lax.top_k(score, topk)
        valid = sel < limit[None, :, None]
        kv_sel = jax.vmap(lambda c, i: c[i])(ckv, sel)
        logits = jnp.einsum('bqhc,bqkc->bqhk', ql, kv_sel)
        logits = jnp.where(valid[:, :, None, :], logits, -jnp.inf)
        p = jax.nn.softmax(logits, axis=-1)
        return jnp.einsum('bqhk,bqkc->bqhc', p, kv_sel)

    o_lat = lax.map(block, (to_blocks(q_idx), to_blocks(w_idx), to_blocks(q_lat), jnp.arange(nblk)))
    o_lat = jnp.moveaxis(o_lat, 0, 1).reshape(Bn, T, B_HEADS, KV_LATENT)
    o = jnp.einsum('bthc,hcd->bthd', o_lat, w_uv)
    return o.reshape(Bn, T, B_WIDTH).astype(zb.dtype)


def _sgu_mixer(xn, w_in, b_in, ln_g, ln_b, w_s, b_s):
    Bn, T, _ = xn.shape
    z = jax.nn.gelu(xn @ w_in + b_in, approximate=False)
    u, v = jnp.split(z, 2, axis=-1)
    v = _layernorm(v, ln_g, ln_b)
    i = jnp.arange(SG_CHUNK)
    mask = (i[None, :] // CHUNK) <= (i[:, None] // CHUNK)
    v = v.reshape(Bn, T // SG_CHUNK, SG_CHUNK, SG_GROUPS, SG_GROUP_DIM)
    s = jnp.einsum('gij,bnjgc->bnigc', jnp.where(mask[None], w_s, 0.0), v) + b_s.T[None, None, :, :, None]
    return u * s.reshape(Bn, T, SG_WIDTH)


def _conv_ffn(xn, w_up, conv_w, conv_b, w_down):
    h = xn @ w_up
    h = lax.conv_general_dilated(
        h, conv_w[:, None, :].astype(h.dtype), window_strides=(1,), padding=[(CONV_W - 1, 0)],
        dimension_numbers=('NWC', 'WIO', 'NWC'), feature_group_count=h.shape[-1]) + conv_b
    gate, up = jnp.split(h, 2, axis=-1)
    return (jax.nn.silu(gate) * up) @ w_down


def setup_inputs(seed: int = 0) -> dict:
    key = jax.random.key(seed)
    ks = iter(jax.random.split(key, 40))
    f32 = jnp.float32

    def nrm(shape, scale):
        return jax.random.normal(next(ks), shape, f32) * scale

    NE, NO = N_EVEN, N_ODD
    return {
        'x': nrm((BATCH, SEQ, D_MODEL), 1.0),
        'norm_g': 1.0 + nrm((DEPTH, 4, D_MODEL), 0.02),
        'ab_w_in': nrm((NE, D_MODEL, AB_IN), D_MODEL ** -0.5),
        'ab_shift_mu': jax.random.uniform(next(ks), (NE, A_IN), f32),
        'rw_w0': jax.random.uniform(next(ks), (NE, A_WIDTH), f32, -6.0, 0.0),
        'rw_w_up': nrm((NE, LORA_W, A_WIDTH), 0.5 * LORA_W ** -0.5),
        'rw_a0': nrm((NE, A_WIDTH), 0.5),
        'rw_a_up': nrm((NE, LORA_A, A_WIDTH), 0.5 * LORA_A ** -0.5),
        'rw_g_up': nrm((NE, LORA_G, A_WIDTH), LORA_G ** -0.5),
        'rw_k_k': 0.85 + nrm((NE, A_WIDTH), 0.02),
        'rw_k_a': 1.0 + nrm((NE, A_WIDTH), 0.02),
        'rw_r_k': nrm((NE, A_HEADS, A_HEAD_DIM), 0.1),
        'rw_gn_g': 1.0 + nrm((NE, A_WIDTH), 0.02),
        'rw_gn_b': nrm((NE, A_WIDTH), 0.02),
        'mla_kv_norm': 1.0 + nrm((NE, KV_LATENT), 0.02),
        'mla_w_uk': nrm((NE, B_HEADS, B_HEAD_DIM, KV_LATENT), B_HEAD_DIM ** -0.5),
        'mla_w_uv': nrm((NE, B_HEADS, KV_LATENT, B_HEAD_DIM), KV_LATENT ** -0.5),
        'idx_k_norm': 1.0 + nrm((NE, IDX_HEAD_DIM), 0.02),
        'ab_w_out': nrm((NE, AB_OUT, D_MODEL), AB_OUT ** -0.5),
        'sg_w_in': nrm((NO, D_MODEL, 2 * SG_WIDTH), D_MODEL ** -0.5),
        'sg_b_in': nrm((NO, 2 * SG_WIDTH), 0.02),
        'sg_ln_g': 1.0 + nrm((NO, SG_WIDTH), 0.02),
        'sg_ln_b': nrm((NO, SG_WIDTH), 0.02),
        'sg_w_s': nrm((NO, SG_GROUPS, SG_CHUNK, SG_CHUNK), SG_CHUNK ** -0.5),
        'sg_b_s': 1.0 + nrm((NO, SG_GROUPS, SG_CHUNK), 0.02),
        'sg_w_out': nrm((NO, SG_WIDTH, D_MODEL), SG_WIDTH ** -0.5),
        'ffn_w_up': nrm((DEPTH, D_MODEL, 2 * D_FF), D_MODEL ** -0.5),
        'ffn_conv_w': nrm((DEPTH, CONV_W, 2 * D_FF), CONV_W ** -0.5),
        'ffn_conv_b': nrm((DEPTH, 2 * D_FF), 0.02),
        'ffn_w_down': nrm((DEPTH, D_FF, D_MODEL), D_FF ** -0.5),
    }


def reference(x, norm_g, ab_w_in, ab_shift_mu, rw_w0, rw_w_up, rw_a0, rw_a_up, rw_g_up,
              rw_k_k, rw_k_a, rw_r_k, rw_gn_g, rw_gn_b, mla_kv_norm, mla_w_uk, mla_w_uv,
              idx_k_norm, ab_w_out, sg_w_in, sg_b_in, sg_ln_g, sg_ln_b, sg_w_s, sg_b_s,
              sg_w_out, ffn_w_up, ffn_conv_w, ffn_conv_b, ffn_w_down):
    h = x
    for layer in range(DEPTH):
        g_pre_mix, g_post_mix, g_pre_ffn, g_post_ffn = norm_g[layer]
        xn = _rmsnorm(h, g_pre_mix)
        if layer % 2 == 0:
            e = layer // 2
            z = xn @ ab_w_in[e]
            y_a = _rwkv7_mixer(z[..., :A_IN], ab_shift_mu[e], rw_w0[e], rw_w_up[e], rw_a0[e],
                               rw_a_up[e], rw_g_up[e], rw_k_k[e], rw_k_a[e], rw_r_k[e],
                               rw_gn_g[e], rw_gn_b[e])
            y_b = _dsa_mixer(z[..., A_IN:], mla_kv_norm[e], mla_w_uk[e], mla_w_uv[e], idx_k_norm[e])
            mix = jnp.concatenate([y_a, y_b], axis=-1) @ ab_w_out[e]
        else:
            o = layer // 2
            mix = _sgu_mixer(xn, sg_w_in[o], sg_b_in[o], sg_ln_g[o], sg_ln_b[o],
                             sg_w_s[o], sg_b_s[o]) @ sg_w_out[o]
        h = h + _rmsnorm(mix, g_post_mix)
        ffn = _conv_ffn(_rmsnorm(h, g_pre_ffn), ffn_w_up[layer], ffn_conv_w[layer],
                        ffn_conv_b[layer], ffn_w_down[layer])
        h = h + _rmsnorm(ffn, g_post_ffn)
    return h
```

```python
import functools

import jax
import jax.numpy as jnp
from jax import lax
from jax.experimental import pallas as pl
from jax.experimental.pallas import tpu as pltpu

F32 = jnp.float32
BF16 = jnp.bfloat16

NORM_EPS = 1e-6
CHUNK = 64
A_HEADS = 16
A_HEAD_DIM = 64
A_WIDTH = A_HEADS * A_HEAD_DIM
LORA_W = 64
LORA_A = 64
LORA_G = 160
A_GN_EPS = 64e-5
B_HEADS = 8
B_HEAD_DIM = 128
B_WIDTH = B_HEADS * B_HEAD_DIM
KV_LATENT = 256
IDX_HEADS = 16
IDX_HEAD_DIM = 64
TOPK_MAX = 256
Q_BLOCK = 128
SG_CHUNK = 128
SG_GROUPS = 8
CONV_W = 3

LANES = 128
BF16_SUBLANES = 16
VMEM_LIMIT = 56 * 1024 * 1024

COL_R = 0
COL_K = A_WIDTH
COL_V = 2 * A_WIDTH
COL_Q = 3 * A_WIDTH
COL_QI = COL_Q + B_WIDTH
COL_CKV = COL_QI + IDX_HEADS * IDX_HEAD_DIM
COL_XG = COL_CKV + KV_LATENT
COL_XWXA = COL_XG + 256
COL_KW = COL_XWXA + LANES
AB_IN_PAD = 6144

RW_CHUNK = 64
NEG_BIG = -1e30


def _rms(x, g):
    ms = jnp.mean(x * x, axis=-1, keepdims=True)
    return x * lax.rsqrt(ms + NORM_EPS) * g


def _dot(a, b):
    return jnp.dot(a, b, preferred_element_type=F32)


def _dot_nt(a, b):
    return lax.dot_general(a, b, (((1,), (1,)), ((), ())), preferred_element_type=F32)


def _dot_tn(a, b):
    return lax.dot_general(a, b, (((0,), (0,)), ((), ())), preferred_element_type=F32)


def _split2(x):
    hi = x.astype(BF16)
    lo = (x - hi.astype(F32)).astype(BF16)
    return hi, lo


def _split3(x):
    hi = x.astype(BF16)
    r1 = x - hi.astype(F32)
    mid = r1.astype(BF16)
    lo = (r1 - mid.astype(F32)).astype(BF16)
    return hi, mid, lo


def _norm_mm_kernel(x_ref, g_ref, w_ref, b_ref, o_ref, xn_ref, *, gelu):
    @pl.when(pl.program_id(1) == 0)
    def _():
        xn_ref[...] = _rms(x_ref[...], g_ref[...]).astype(BF16)

    y = _dot(xn_ref[...], w_ref[...])
    if gelu:
        y = y + b_ref[...]
        y = 0.5 * y * (1.0 + lax.erf(y * 0.7071067811865476))
    o_ref[...] = y.astype(o_ref.dtype)


def _norm_matmul(x, g, w, b, *, gelu, tm=512, tn=1024, out_dtype=F32):
    m, k = x.shape
    n = w.shape[1]
    return pl.pallas_call(
        functools.partial(_norm_mm_kernel, gelu=gelu),
        grid=(m // tm, n // tn),
        in_specs=[
            pl.BlockSpec((tm, k), lambda i, j: (i, 0)),
            pl.BlockSpec((1, k), lambda i, j: (0, 0)),
            pl.BlockSpec((k, tn), lambda i, j: (0, j)),
            pl.BlockSpec((1, tn), lambda i, j: (0, j)),
        ],
        out_specs=pl.BlockSpec((tm, tn), lambda i, j: (i, j)),
        out_shape=jax.ShapeDtypeStruct((m, n), out_dtype),
        scratch_shapes=[pltpu.VMEM((tm, k), BF16)],
        compiler_params=pltpu.CompilerParams(
            dimension_semantics=("parallel", "arbitrary"), vmem_limit_bytes=VMEM_LIMIT),
        name="norm_matmul",
    )(x, g, w, b)


FFN_HALO = BF16_SUBLANES


def _ffn_kernel(x_ref, xp_ref, gpre_ref, wg_ref, wu_ref, cwg_ref, cwu_ref, cbg_ref, cbu_ref,
                wd_ref, gpost_ref, o_ref, xn_ref, hg_ref, hu_ref, acc_ref, *, tm, tiles_per_seq):
    i = pl.program_id(0)
    j = pl.program_id(1)

    @pl.when(j == 0)
    def _():
        xn_ref[FFN_HALO:, :] = _rms(x_ref[...], gpre_ref[...]).astype(BF16)
        keep = jnp.where(i % tiles_per_seq != 0, 1.0, 0.0)
        xn_ref[:FFN_HALO, :] = (_rms(xp_ref[...], gpre_ref[...]) * keep).astype(BF16)
        acc_ref[...] = jnp.zeros_like(acc_ref)

    xn = xn_ref[...]
    hg_ref[...] = _dot(xn, wg_ref[...])
    hu_ref[...] = _dot(xn, wu_ref[...])

    def conv(h_ref, cw_ref, cb_ref):
        cw = cw_ref[...]
        return (h_ref[pl.ds(FFN_HALO, tm), :] * cw[2:3, :]
                + h_ref[pl.ds(FFN_HALO - 1, tm), :] * cw[1:2, :]
                + h_ref[pl.ds(FFN_HALO - 2, tm), :] * cw[0:1, :]
                + cb_ref[...])

    gate = conv(hg_ref, cwg_ref, cbg_ref)
    up = conv(hu_ref, cwu_ref, cbu_ref)
    act = gate * jax.nn.sigmoid(gate) * up
    acc_ref[...] += _dot(act.astype(BF16), wd_ref[...])

    @pl.when(j == pl.num_programs(1) - 1)
    def _():
        o_ref[...] = x_ref[...] + _rms(acc_ref[...], gpost_ref[...])


def _ffn(h, g_pre, g_post, w_up, conv_w, conv_b, w_down, *, seq, tm=512, tf=512):
    m, d = h.shape
    d_ff = w_down.shape[0]
    nf = d_ff // tf
    halo_blocks = tm // FFN_HALO
    return pl.pallas_call(
        functools.partial(_ffn_kernel, tm=tm, tiles_per_seq=seq // tm),
        grid=(m // tm, nf),
        in_specs=[
            pl.BlockSpec((tm, d), lambda i, j: (i, 0)),
            pl.BlockSpec((FFN_HALO, d), lambda i, j: (jnp.maximum(i * halo_blocks - 1, 0), 0)),
            pl.BlockSpec((1, d), lambda i, j: (0, 0)),
            pl.BlockSpec((d, tf), lambda i, j: (0, j)),
            pl.BlockSpec((d, tf), lambda i, j: (0, j + nf)),
            pl.BlockSpec((CONV_W, tf), lambda i, j: (0, j)),
            pl.BlockSpec((CONV_W, tf), lambda i, j: (0, j + nf)),
            pl.BlockSpec((1, tf), lambda i, j: (0, j)),
            pl.BlockSpec((1, tf), lambda i, j: (0, j + nf)),
            pl.BlockSpec((tf, d), lambda i, j: (j, 0)),
            pl.BlockSpec((1, d), lambda i, j: (0, 0)),
        ],
        out_specs=pl.BlockSpec((tm, d), lambda i, j: (i, 0)),
        out_shape=jax.ShapeDtypeStruct((m, d), F32),
        scratch_shapes=[
            pltpu.VMEM((tm + FFN_HALO, d), BF16),
            pltpu.VMEM((tm + FFN_HALO, tf), F32),
            pltpu.VMEM((tm + FFN_HALO, tf), F32),
            pltpu.VMEM((tm, d), F32),
        ],
        compiler_params=pltpu.CompilerParams(
            dimension_semantics=("parallel", "arbitrary"), vmem_limit_bytes=VMEM_LIMIT),
        name="conv_ffn",
    )(h, h, g_pre, w_up, w_up, conv_w, conv_w, conv_b, conv_b, w_down, g_post)


def _outproj_kernel(ya_ref, yb_ref, w_ref, h_ref, g_ref, o_ref):
    ka = ya_ref.shape[1]
    mix = _dot(ya_ref[...], w_ref[:ka, :]) + _dot(yb_ref[...], w_ref[ka:, :])
    o_ref[...] = h_ref[...] + _rms(mix, g_ref[...])


def _outproj(ya, yb, w, h, g, *, tm=512):
    m, d = h.shape
    return pl.pallas_call(
        _outproj_kernel,
        grid=(m // tm,),
        in_specs=[
            pl.BlockSpec((tm, ya.shape[1]), lambda i: (i, 0)),
            pl.BlockSpec((tm, yb.shape[1]), lambda i: (i, 0)),
            pl.BlockSpec(w.shape, lambda i: (0, 0)),
            pl.BlockSpec((tm, d), lambda i: (i, 0)),
            pl.BlockSpec((1, d), lambda i: (0, 0)),
        ],
        out_specs=pl.BlockSpec((tm, d), lambda i: (i, 0)),
        out_shape=jax.ShapeDtypeStruct((m, d), F32),
        compiler_params=pltpu.CompilerParams(
            dimension_semantics=("parallel",), vmem_limit_bytes=VMEM_LIMIT),
        name="outproj",
    )(ya, yb, w, h, g)


def _sgu_kernel(u_ref, v_ref, lng_ref, lnb_ref, ws_ref, bs_ref, wo_ref, h_ref, g_ref, o_ref,
                vln_ref, gated_ref, *, tm):
    v = v_ref[...]
    mu = jnp.mean(v, axis=-1, keepdims=True)
    dv = v - mu
    var = jnp.mean(dv * dv, axis=-1, keepdims=True)
    vln_ref[...] = (dv * lax.rsqrt(var + 1e-5) * lng_ref[...] + lnb_ref[...]).astype(BF16)

    gw = v.shape[1] // SG_GROUPS
    ii = lax.broadcasted_iota(jnp.int32, (SG_CHUNK, SG_CHUNK), 0)
    jj = lax.broadcasted_iota(jnp.int32, (SG_CHUNK, SG_CHUNK), 1)
    visible = (jj // CHUNK) <= (ii // CHUNK)
    for g in range(SG_GROUPS):
        wm = jnp.where(visible, ws_ref[g], 0.0).astype(BF16)
        bcol = bs_ref[:, g:g + 1]
        cols = slice(g * gw, (g + 1) * gw)
        for c in range(tm // SG_CHUNK):
            rows = slice(c * SG_CHUNK, (c + 1) * SG_CHUNK)
            s = _dot(wm, vln_ref[rows, cols]) + bcol
            gated_ref[rows, cols] = (u_ref[rows, cols] * s).astype(BF16)

    mix = _dot(gated_ref[...], wo_ref[...])
    o_ref[...] = h_ref[...] + _rms(mix, g_ref[...])


def _sgu(z, ln_g, ln_b, w_s, b_s_t, w_out, h, g, *, tm=256):
    m, d = h.shape
    return pl.pallas_call(
        functools.partial(_sgu_kernel, tm=tm),
        grid=(m // tm,),
        in_specs=[
            pl.BlockSpec((tm, d), lambda i: (i, 0)),
            pl.BlockSpec((tm, d), lambda i: (i, 1)),
            pl.BlockSpec((1, d), lambda i: (0, 0)),
            pl.BlockSpec((1, d), lambda i: (0, 0)),
            pl.BlockSpec(w_s.shape, lambda i: (0, 0, 0)),
            pl.BlockSpec(b_s_t.shape, lambda i: (0, 0)),
            pl.BlockSpec(w_out.shape, lambda i: (0, 0)),
            pl.BlockSpec((tm, d), lambda i: (i, 0)),
            pl.BlockSpec((1, d), lambda i: (0, 0)),
        ],
        out_specs=pl.BlockSpec((tm, d), lambda i: (i, 0)),
        out_shape=jax.ShapeDtypeStruct((m, d), F32),
        scratch_shapes=[pltpu.VMEM((tm, d), BF16), pltpu.VMEM((tm, d), BF16)],
        compiler_params=pltpu.CompilerParams(
            dimension_semantics=("parallel",), vmem_limit_bytes=VMEM_LIMIT),
        name="sgu",
    )(z, z, ln_g, ln_b, w_s, b_s_t, w_out, h, g)


RW_PP_ROWS = 16
(PP_MU_R, PP_MU_K, PP_MU_V, PP_W0, PP_A0, PP_KK, PP_KA, PP_RK, PP_GNG, PP_GNB) = range(10)
RW_CARRY = 8


def _rwkv_kernel(r_ref, k_ref, v_ref, xx_ref, xg_ref, pp_ref, mux_ref, mug_ref, wup_ref, aup_ref,
                 gup_ref, y_ref,
                 rb_ref, kb_ref, vb_ref, xb_ref, gb_ref,
                 r_s, k_s, v_s, lw_s, kk_s, be_s, g_s, bonus_s, y_s, state_ref, *, tt):
    L = RW_CHUNK
    tb = pl.program_id(2)
    pp = pp_ref[...]
    row = lambda i: pp[i:i + 1, :]

    lane = lax.broadcasted_iota(jnp.int32, (1, LANES), 1)
    head_masks = (lane < A_HEAD_DIM, lane >= A_HEAD_DIM)
    li = lax.broadcasted_iota(jnp.int32, (LANES, LANES), 0)
    lj = lax.broadcasted_iota(jnp.int32, (LANES, LANES), 1)
    same_head = (li // A_HEAD_DIM) == (lj // A_HEAD_DIM)
    ones_bd = jnp.where(same_head, 1.0, 0.0).astype(BF16)

    def seg_sum(x):
        hi, lo = _split2(x)
        return _dot(hi, ones_bd) + _dot(lo, ones_bd)

    @pl.when(tb == 0)
    def _():
        for b in (rb_ref, kb_ref, vb_ref, xb_ref, gb_ref):
            b[0:RW_CARRY, :] = jnp.zeros((RW_CARRY, b.shape[1]), F32)
        state_ref[...] = jnp.zeros_like(state_ref)

    def shift_mix(x_ref, buf_ref, mu):
        x = x_ref[...]
        buf_ref[RW_CARRY:, :] = x
        prev = buf_ref[pl.ds(RW_CARRY - 1, tt), :]
        return x + (prev - x) * mu

    r = shift_mix(r_ref, rb_ref, row(PP_MU_R))
    k = shift_mix(k_ref, kb_ref, row(PP_MU_K))
    v = shift_mix(v_ref, vb_ref, row(PP_MU_V))
    xx = shift_mix(xx_ref, xb_ref, mux_ref[...])
    xg = shift_mix(xg_ref, gb_ref, mug_ref[...])
    for src, buf in ((r_ref, rb_ref), (k_ref, kb_ref), (v_ref, vb_ref), (xx_ref, xb_ref),
                     (xg_ref, gb_ref)):
        buf[0:RW_CARRY, :] = src[tt - RW_CARRY:tt, :]

    w_pre = row(PP_W0) + _dot(jnp.tanh(xx).astype(BF16), wup_ref[...])
    log_w = -(jnp.maximum(-w_pre, 0.0) + jnp.log1p(jnp.exp(-jnp.abs(w_pre)))) - 0.5
    lw_s[...] = -jnp.exp(log_w)
    a = jax.nn.sigmoid(row(PP_A0) + _dot(xx.astype(BF16), aup_ref[...]))
    g_s[...] = _dot(jax.nn.sigmoid(xg).astype(BF16), gup_ref[...])
    kk = k * row(PP_KK)
    kk = kk * lax.rsqrt(seg_sum(kk * kk) + 1e-12)
    k = k * (1.0 + (a - 1.0) * row(PP_KA))
    bonus_s[...] = seg_sum(r * k * row(PP_RK)) * v
    r_s[...] = r
    k_s[...] = k
    v_s[...] = v
    kk_s[...] = kk
    be_s[...] = kk * a

    ci = lax.broadcasted_iota(jnp.int32, (L, L), 0)
    cj = lax.broadcasted_iota(jnp.int32, (L, L), 1)
    tril_incl = cj <= ci
    tril_strict = cj < ci
    tri_ones = jnp.where(tril_incl, 1.0, 0.0).astype(BF16)
    eye = jnp.where(ci == cj, 1.0, 0.0).astype(F32)

    def tri_inverse(n):
        t = eye + n
        p = n
        for _ in range(5):
            p16 = p.astype(BF16)
            p = _dot(p16, p16)
            t = t + _dot(t.astype(BF16), p.astype(BF16))
        return t

    def chunk(c, state):
        sl = pl.ds(pl.multiple_of(c * L, L), L)
        lw = lw_s[sl, :]
        rr = r_s[sl, :]
        kc = k_s[sl, :]
        vc = v_s[sl, :]
        kkc = kk_s[sl, :]
        bec = be_s[sl, :]
        h3 = _split3(lw)
        cs = _dot(tri_ones, h3[0]) + _dot(tri_ones, h3[1]) + _dot(tri_ones, h3[2])
        e_pos = jnp.exp(cs)
        e_prev = jnp.exp(cs - lw)
        e_neg = jnp.exp(-cs)
        ab = (-kkc * e_prev).astype(BF16)
        rb = (rr * e_pos).astype(BF16)
        bt = bec * e_neg
        kt = kc * e_neg
        lam_l = e_pos[L - 1:L, :]
        ar = jnp.concatenate([ab, rb], axis=0)
        v16 = vc.astype(BF16)
        s16 = state.astype(BF16)
        x = _dot_nt(ab, s16)
        y = _dot_nt(rb, s16)
        u = jnp.zeros((L, LANES), F32)
        for mh in head_masks:
            bh = jnp.where(mh, bt, 0.0).astype(BF16)
            kh = jnp.where(mh, kt, 0.0).astype(BF16)
            gb = _dot_nt(ar, bh)
            gk = _dot_nt(ar, kh)
            a_ab = jnp.where(tril_strict, gb[:L, :], 0.0)
            a_rb = jnp.where(tril_incl, gb[L:, :], 0.0)
            a_ak = jnp.where(tril_strict, gk[:L, :], 0.0)
            a_rk = jnp.where(tril_incl, gk[L:, :], 0.0)
            t = tri_inverse(a_ab)
            rhs = x + _dot(a_ak.astype(BF16), v16)
            uh = _dot(t.astype(BF16), rhs.astype(BF16))
            yh = _dot(a_rb.astype(BF16), uh.astype(BF16)) + _dot(a_rk.astype(BF16), v16)
            u = u + jnp.where(mh, uh, 0.0)
            y = y + jnp.where(mh, yh, 0.0)
        y_s[sl, :] = y
        ds = _dot_tn(u.astype(BF16), bt.astype(BF16)) + _dot_tn(v16, kt.astype(BF16))
        return (state + jnp.where(same_head, ds, 0.0)) * lam_l

    state_ref[...] = lax.fori_loop(0, tt // L, chunk, state_ref[...])

    y = y_s[...]
    inv_n = 1.0 / A_HEAD_DIM
    mean = seg_sum(y) * inv_n
    d = y - mean
    var = seg_sum(d * d) * inv_n
    yn = d * lax.rsqrt(var + A_GN_EPS) * row(PP_GNG) + row(PP_GNB)
    y_ref[...] = ((yn + bonus_s[...]) * g_s[...]).astype(y_ref.dtype)


def _rwkv(z3, pp, mu_xx, mu_xg, wup, aup, gup, *, tt=1024):
    bsz, seq, _ = z3.shape
    pairs = A_WIDTH // LANES
    cb = lambda col, w: col // w
    col_spec = lambda col0: pl.BlockSpec((None, tt, LANES), lambda b, p, t: (b, t, cb(col0, LANES) + p))
    buf = lambda w: pltpu.VMEM((tt + RW_CARRY, w), F32)
    big = pltpu.VMEM((tt, LANES), F32)
    return pl.pallas_call(
        functools.partial(_rwkv_kernel, tt=tt),
        grid=(bsz, pairs, seq // tt),
        in_specs=[
            col_spec(COL_R), col_spec(COL_K), col_spec(COL_V),
            pl.BlockSpec((None, tt, LANES), lambda b, p, t: (b, t, cb(COL_XWXA, LANES))),
            pl.BlockSpec((None, tt, 256), lambda b, p, t: (b, t, cb(COL_XG, 256))),
            pl.BlockSpec((RW_PP_ROWS, LANES), lambda b, p, t: (0, p)),
            pl.BlockSpec((1, LANES), lambda b, p, t: (0, 0)),
            pl.BlockSpec((1, 256), lambda b, p, t: (0, 0)),
            pl.BlockSpec((LANES, LANES), lambda b, p, t: (0, p)),
            pl.BlockSpec((LANES, LANES), lambda b, p, t: (0, p)),
            pl.BlockSpec((256, LANES), lambda b, p, t: (0, p)),
        ],
        out_specs=pl.BlockSpec((None, tt, LANES), lambda b, p, t: (b, t, p)),
        out_shape=jax.ShapeDtypeStruct((bsz, seq, A_WIDTH), BF16),
        scratch_shapes=[buf(LANES), buf(LANES), buf(LANES), buf(LANES), buf(256)]
        + [big] * 9 + [pltpu.VMEM((LANES, LANES), F32)],
        compiler_params=pltpu.CompilerParams(
            dimension_semantics=("parallel", "parallel", "arbitrary"), vmem_limit_bytes=VMEM_LIMIT),
        name="rwkv7",
    )(z3, z3, z3, z3, z3, pp, mu_xx, mu_xg, wup, aup, gup)


DSA_TK = 512


def _dsa_kernel(q_ref, qi_ref, kwq_ref, ckv_ref, kwf_ref, kvn_ref, ikn_ref, wuk_ref, wuv_ref,
                o_ref, ckv_s, ke_s, ko_s, key_s, qlat_s, m_s, l_s, acc_s, *, seq):
    j = pl.program_id(1)
    tk = DSA_TK
    qb = Q_BLOCK
    lane = lax.broadcasted_iota(jnp.int32, (1, LANES), 1)

    @pl.when(j == 0)
    def _():
        def prep(t, carry):
            sl = pl.ds(pl.multiple_of(t * tk, tk), tk)
            ckv_s[sl, :] = _rms(ckv_ref[sl, :], kvn_ref[...]).astype(BF16)
            kw = kwf_ref[sl, :]
            ksq = jnp.where(lane < IDX_HEAD_DIM, kw * kw, 0.0)
            ms = jnp.sum(ksq, axis=-1, keepdims=True) * (1.0 / IDX_HEAD_DIM)
            kn = kw * lax.rsqrt(ms + NORM_EPS) * ikn_ref[...]
            ke_s[sl, :] = kn.astype(BF16)
            ko_s[sl, :] = pltpu.roll(kn, IDX_HEAD_DIM, axis=1).astype(BF16)
            return carry
        lax.fori_loop(0, seq // tk, prep, 0)

    n_tiles = (j * qb + qb + tk - 1) // tk
    qrow = lax.broadcasted_iota(jnp.int32, (qb, 1), 0)
    limit = j * qb + (qrow // CHUNK + 1) * CHUNK

    w_all = kwq_ref[...] * ((IDX_HEADS ** -0.5) * (IDX_HEAD_DIM ** -0.5))
    qi16 = [qi_ref[:, p * LANES:(p + 1) * LANES].astype(BF16) for p in range(IDX_HEADS // 2)]
    w_cols = [w_all[:, IDX_HEAD_DIM + h:IDX_HEAD_DIM + h + 1] for h in range(IDX_HEADS)]

    def score_tile(t, carry):
        sl = pl.ds(pl.multiple_of(t * tk, tk), tk)
        ke = ke_s[sl, :]
        ko = ko_s[sl, :]
        sc = jnp.zeros((qb, tk), F32)
        for p in range(IDX_HEADS // 2):
            sc = sc + jnp.maximum(_dot_nt(qi16[p], ke), 0.0) * w_cols[2 * p]
            sc = sc + jnp.maximum(_dot_nt(qi16[p], ko), 0.0) * w_cols[2 * p + 1]
        kpos = t * tk + lax.broadcasted_iota(jnp.int32, (1, tk), 1)
        sc = jnp.where(kpos < limit, sc, -jnp.inf)
        bits = pltpu.bitcast(sc, jnp.int32)
        key_s[t] = bits ^ ((bits >> 31) & 0x7FFFFFFF)
        return carry

    lax.fori_loop(0, n_tiles, score_tile, 0)

    def count_ge(cand):
        def body(t, cnt):
            ge = jnp.where(key_s[t] >= cand, 1.0, 0.0)
            return cnt + jnp.sum(ge, axis=-1, keepdims=True)
        return lax.fori_loop(0, n_tiles, body, jnp.zeros((qb, 1), F32))

    int_min = jnp.int32(-2 ** 31)
    topk = float(TOPK_MAX)
    theta = jnp.where(count_ge(jnp.zeros((qb, 1), jnp.int32)) >= topk, 0, int_min)

    def bit_step(i, theta):
        cand = theta | (jnp.int32(1) << (30 - i))
        return jnp.where(count_ge(cand) >= topk, cand, theta)

    theta = lax.fori_loop(0, 31, bit_step, theta)

    for h in range(B_HEADS):
        qh = q_ref[:, h * B_HEAD_DIM:(h + 1) * B_HEAD_DIM].astype(BF16)
        qlat_s[h * qb:(h + 1) * qb, :] = (_dot(qh, wuk_ref[h]) * (B_HEAD_DIM ** -0.5)).astype(BF16)

    m_s[...] = jnp.full_like(m_s, NEG_BIG)
    l_s[...] = jnp.zeros_like(l_s)
    acc_s[...] = jnp.zeros_like(acc_s)

    def attend(t, carry):
        sl = pl.ds(pl.multiple_of(t * tk, tk), tk)
        ckv = ckv_s[sl, :]
        kpos = t * tk + lax.broadcasted_iota(jnp.int32, (1, tk), 1)
        sel = (key_s[t] >= theta) & (kpos < limit)
        logits = _dot_nt(qlat_s[...], ckv).reshape(B_HEADS, qb, tk)
        logits = jnp.where(sel[None, :, :], logits, NEG_BIG)
        m_old = m_s[...]
        m_new = jnp.maximum(m_old, jnp.max(logits, axis=-1, keepdims=True))
        alpha = jnp.exp(m_old - m_new)
        p = jnp.exp(logits - m_new)
        l_s[...] = alpha * l_s[...] + jnp.sum(p, axis=-1, keepdims=True)
        pv = _dot(p.reshape(B_HEADS * qb, tk).astype(BF16), ckv).reshape(B_HEADS, qb, KV_LATENT)
        acc_s[...] = alpha * acc_s[...] + pv
        m_s[...] = m_new
        return carry

    lax.fori_loop(0, n_tiles, attend, 0)

    o_lat = acc_s[...] / l_s[...]
    for h in range(B_HEADS):
        o_ref[:, h * B_HEAD_DIM:(h + 1) * B_HEAD_DIM] = _dot(
            o_lat[h].astype(BF16), wuv_ref[h]).astype(o_ref.dtype)


def _dsa(z3, kv_norm, ikn_pad, w_uk, w_uv):
    bsz, seq, _ = z3.shape
    qb = Q_BLOCK
    return pl.pallas_call(
        functools.partial(_dsa_kernel, seq=seq),
        grid=(bsz, seq // qb),
        in_specs=[
            pl.BlockSpec((None, qb, B_WIDTH), lambda b, j: (b, j, COL_Q // B_WIDTH)),
            pl.BlockSpec((None, qb, B_WIDTH), lambda b, j: (b, j, COL_QI // B_WIDTH)),
            pl.BlockSpec((None, qb, LANES), lambda b, j: (b, j, COL_KW // LANES)),
            pl.BlockSpec((None, seq, KV_LATENT), lambda b, j: (b, 0, COL_CKV // KV_LATENT)),
            pl.BlockSpec((None, seq, LANES), lambda b, j: (b, 0, COL_KW // LANES)),
            pl.BlockSpec((1, KV_LATENT), lambda b, j: (0, 0)),
            pl.BlockSpec((1, LANES), lambda b, j: (0, 0)),
            pl.BlockSpec(w_uk.shape, lambda b, j: (0, 0, 0)),
            pl.BlockSpec(w_uv.shape, lambda b, j: (0, 0, 0)),
        ],
        out_specs=pl.BlockSpec((None, qb, B_WIDTH), lambda b, j: (b, j, 0)),
        out_shape=jax.ShapeDtypeStruct((bsz, seq, B_WIDTH), BF16),
        scratch_shapes=[
            pltpu.VMEM((seq, KV_LATENT), BF16),
            pltpu.VMEM((seq, LANES), BF16),
            pltpu.VMEM((seq, LANES), BF16),
            pltpu.VMEM((seq // DSA_TK, qb, DSA_TK), jnp.int32),
            pltpu.VMEM((B_HEADS * qb, KV_LATENT), BF16),
            pltpu.VMEM((B_HEADS, qb, 1), F32),
            pltpu.VMEM((B_HEADS, qb, 1), F32),
            pltpu.VMEM((B_HEADS, qb, KV_LATENT), F32),
        ],
        compiler_params=pltpu.CompilerParams(
            dimension_semantics=("parallel", "arbitrary"), vmem_limit_bytes=VMEM_LIMIT),
        name="dsa",
    )(z3, z3, z3, z3, z3, kv_norm, ikn_pad, w_uk, w_uv)


def _pad_cols(w, width):
    return jnp.pad(w, ((0, 0), (0, width - w.shape[1])))


def _layer0_params(ab_w_in, mu, w0, w_up, a0, a_up, g_up, k_k, k_a, r_k, gn_g, gn_b, idx_k_norm):
    a_in = 3 * A_WIDTH + LORA_W + LORA_A + LORA_G
    o = 3 * A_WIDTH
    w_r, w_k, w_v = ab_w_in[:, :A_WIDTH], ab_w_in[:, A_WIDTH:2 * A_WIDTH], ab_w_in[:, 2 * A_WIDTH:o]
    w_xwxa = ab_w_in[:, o:o + LORA_W + LORA_A]
    w_xg = ab_w_in[:, o + LORA_W + LORA_A:a_in]
    wb = ab_w_in[:, a_in:]
    o1 = B_WIDTH
    o2 = o1 + KV_LATENT
    o3 = o2 + IDX_HEADS * IDX_HEAD_DIM
    w_q, w_ckv, w_qi, w_kw = wb[:, :o1], wb[:, o1:o2], wb[:, o2:o3], wb[:, o3:]
    w_in = jnp.concatenate(
        [w_r, w_k, w_v, w_q, w_qi, w_ckv, _pad_cols(w_xg, 256), w_xwxa, _pad_cols(w_kw, LANES)], axis=1)
    w_in = _pad_cols(w_in, AB_IN_PAD).astype(BF16)

    rows = [mu[:A_WIDTH], mu[A_WIDTH:2 * A_WIDTH], mu[2 * A_WIDTH:o], w0, a0, k_k, k_a,
            r_k.reshape(-1), gn_g, gn_b]
    pp = jnp.stack(rows + [jnp.zeros_like(w0)] * (RW_PP_ROWS - len(rows)), axis=0)
    mu_xx = mu[o:o + LORA_W + LORA_A][None, :]
    mu_xg = jnp.pad(mu[o + LORA_W + LORA_A:a_in], (0, 256 - LORA_G))[None, :]
    wup = jnp.pad(w_up, ((0, LANES - LORA_W), (0, 0))).astype(BF16)
    aup = jnp.pad(a_up, ((LORA_W, 0), (0, 0))).astype(BF16)
    gup = jnp.pad(g_up, ((0, 256 - LORA_G), (0, 0))).astype(BF16)
    ikn = jnp.pad(idx_k_norm, (0, LANES - IDX_HEAD_DIM))[None, :]
    return w_in, pp, mu_xx, mu_xg, wup, aup, gup, ikn


def kernel(x, norm_g, ab_w_in, ab_shift_mu, rw_w0, rw_w_up, rw_a0, rw_a_up, rw_g_up, rw_k_k, rw_k_a, rw_r_k, rw_gn_g, rw_gn_b, mla_kv_norm, mla_w_uk, mla_w_uv, idx_k_norm, ab_w_out, sg_w_in, sg_b_in, sg_ln_g, sg_ln_b, sg_w_s, sg_b_s, sg_w_out, ffn_w_up, ffn_conv_w, ffn_conv_b, ffn_w_down):
    bsz, seq, d = x.shape
    n = bsz * seq
    depth = norm_g.shape[0]
    h = x.reshape(n, d)
    for layer in range(depth):
        g_pre_mix, g_post_mix, g_pre_ffn, g_post_ffn = [norm_g[layer, i][None, :] for i in range(4)]
        if layer % 2 == 0:
            e = layer // 2
            w_in, pp, mu_xx, mu_xg, wup, aup, gup, ikn = _layer0_params(
                ab_w_in[e], ab_shift_mu[e], rw_w0[e], rw_w_up[e], rw_a0[e], rw_a_up[e], rw_g_up[e],
                rw_k_k[e], rw_k_a[e], rw_r_k[e], rw_gn_g[e], rw_gn_b[e], idx_k_norm[e])
            z = _norm_matmul(h, g_pre_mix, w_in, jnp.zeros((1, AB_IN_PAD), F32), gelu=False)
            z3 = z.reshape(bsz, seq, AB_IN_PAD)
            y_a = _rwkv(z3, pp, mu_xx, mu_xg, wup, aup, gup)
            y_b = _dsa(z3, mla_kv_norm[e][None, :], ikn, mla_w_uk[e].astype(BF16),
                       mla_w_uv[e].astype(BF16))
            h = _outproj(y_a.reshape(n, A_WIDTH), y_b.reshape(n, B_WIDTH),
                         ab_w_out[e].astype(BF16), h, g_post_mix)
        else:
            o = layer // 2
            z = _norm_matmul(h, g_pre_mix, sg_w_in[o].astype(BF16), sg_b_in[o][None, :], gelu=True)
            h = _sgu(z, sg_ln_g[o][None, :], sg_ln_b[o][None, :], sg_w_s[o], sg_b_s[o].T,
                     sg_w_out[o].astype(BF16), h, g_post_mix)
        h = _ffn(h, g_pre_ffn, g_post_ffn, ffn_w_up[layer].astype(BF16), ffn_conv_w[layer],
                 ffn_conv_b[layer][None, :], ffn_w_down[layer].astype(BF16), seq=seq)
    return h.reshape(bsz, seq, d)
```

```python
import functools

import jax
import jax.numpy as jnp
from jax import lax
from jax.experimental import pallas as pl
from jax.experimental.pallas import tpu as pltpu

F32 = jnp.float32
BF16 = jnp.bfloat16

NORM_EPS = 1e-6
CHUNK = 64
A_HEADS = 16
A_HEAD_DIM = 64
A_WIDTH = A_HEADS * A_HEAD_DIM
LORA_W = 64
LORA_A = 64
LORA_G = 160
A_GN_EPS = 64e-5
B_HEADS = 8
B_HEAD_DIM = 128
B_WIDTH = B_HEADS * B_HEAD_DIM
KV_LATENT = 256
IDX_HEADS = 16
IDX_HEAD_DIM = 64
TOPK_MAX = 256
Q_BLOCK = 128
SG_CHUNK = 128
SG_GROUPS = 8
CONV_W = 3

LANES = 128
BF16_SUBLANES = 16
VMEM_LIMIT = 56 * 1024 * 1024

COL_R = 0
COL_K = A_WIDTH
COL_V = 2 * A_WIDTH
COL_Q = 3 * A_WIDTH
COL_QI = COL_Q + B_WIDTH
COL_CKV = COL_QI + IDX_HEADS * IDX_HEAD_DIM
COL_XG = COL_CKV + KV_LATENT
COL_XWXA = COL_XG + 256
COL_KW = COL_XWXA + LANES
AB_IN_PAD = 6144

RW_CHUNK = 64
NEG_BIG = -1e30


def _rms(x, g):
    ms = jnp.mean(x * x, axis=-1, keepdims=True)
    return x * lax.rsqrt(ms + NORM_EPS) * g


def _dot(a, b):
    return jnp.dot(a, b, preferred_element_type=F32)


def _dot_nt(a, b):
    return lax.dot_general(a, b, (((1,), (1,)), ((), ())), preferred_element_type=F32)


def _dot_tn(a, b):
    return lax.dot_general(a, b, (((0,), (0,)), ((), ())), preferred_element_type=F32)


def _split2(x):
    hi = x.astype(BF16)
    lo = (x - hi.astype(F32)).astype(BF16)
    return hi, lo


def _split3(x):
    hi = x.astype(BF16)
    r1 = x - hi.astype(F32)
    mid = r1.astype(BF16)
    lo = (r1 - mid.astype(F32)).astype(BF16)
    return hi, mid, lo


def _norm_mm_kernel(x_ref, g_ref, w_ref, b_ref, o_ref, xn_ref, *, gelu):
    @pl.when(pl.program_id(1) == 0)
    def _():
        xn_ref[...] = _rms(x_ref[...], g_ref[...]).astype(BF16)

    y = _dot(xn_ref[...], w_ref[...])
    if gelu:
        y = y + b_ref[...]
        y = 0.5 * y * (1.0 + lax.erf(y * 0.7071067811865476))
    o_ref[...] = y.astype(o_ref.dtype)


def _norm_matmul(x, g, w, b, *, gelu, tm=512, tn=1024, out_dtype=F32):
    m, k = x.shape
    n = w.shape[1]
    return pl.pallas_call(
        functools.partial(_norm_mm_kernel, gelu=gelu),
        grid=(m // tm, n // tn),
        in_specs=[
            pl.BlockSpec((tm, k), lambda i, j: (i, 0)),
            pl.BlockSpec((1, k), lambda i, j: (0, 0)),
            pl.BlockSpec((k, tn), lambda i, j: (0, j)),
            pl.BlockSpec((1, tn), lambda i, j: (0, j)),
        ],
        out_specs=pl.BlockSpec((tm, tn), lambda i, j: (i, j)),
        out_shape=jax.ShapeDtypeStruct((m, n), out_dtype),
        scratch_shapes=[pltpu.VMEM((tm, k), BF16)],
        compiler_params=pltpu.CompilerParams(
            dimension_semantics=("parallel", "arbitrary"), vmem_limit_bytes=VMEM_LIMIT),
        name="norm_matmul",
    )(x, g, w, b)


FFN_HALO = BF16_SUBLANES


def _ffn_kernel(x_ref, xp_ref, gpre_ref, wg_ref, wu_ref, cwg_ref, cwu_ref, cbg_ref, cbu_ref,
                wd_ref, gpost_ref, o_ref, xn_ref, hg_ref, hu_ref, acc_ref, *, tm, tiles_per_seq):
    i = pl.program_id(0)
    j = pl.program_id(1)

    @pl.when(j == 0)
    def _():
        xn_ref[FFN_HALO:, :] = _rms(x_ref[...], gpre_ref[...]).astype(BF16)
        keep = jnp.where(i % tiles_per_seq != 0, 1.0, 0.0)
        xn_ref[:FFN_HALO, :] = (_rms(xp_ref[...], gpre_ref[...]) * keep).astype(BF16)
        acc_ref[...] = jnp.zeros_like(acc_ref)

    xn = xn_ref[...]
    hg_ref[...] = _dot(xn, wg_ref[...])
    hu_ref[...] = _dot(xn, wu_ref[...])

    def conv(h_ref, cw_ref, cb_ref):
        cw = cw_ref[...]
        return (h_ref[pl.ds(FFN_HALO, tm), :] * cw[2:3, :]
                + h_ref[pl.ds(FFN_HALO - 1, tm), :] * cw[1:2, :]
                + h_ref[pl.ds(FFN_HALO - 2, tm), :] * cw[0:1, :]
                + cb_ref[...])

    gate = conv(hg_ref, cwg_ref, cbg_ref)
    up = conv(hu_ref, cwu_ref, cbu_ref)
    act = gate * jax.nn.sigmoid(gate) * up
    acc_ref[...] += _dot(act.astype(BF16), wd_ref[...])

    @pl.when(j == pl.num_programs(1) - 1)
    def _():
        o_ref[...] = x_ref[...] + _rms(acc_ref[...], gpost_ref[...])


def _ffn(h, g_pre, g_post, w_up, conv_w, conv_b, w_down, *, seq, tm=512, tf=512):
    m, d = h.shape
    d_ff = w_down.shape[0]
    nf = d_ff // tf
    halo_blocks = tm // FFN_HALO
    return pl.pallas_call(
        functools.partial(_ffn_kernel, tm=tm, tiles_per_seq=seq // tm),
        grid=(m // tm, nf),
        in_specs=[
            pl.BlockSpec((tm, d), lambda i, j: (i, 0)),
            pl.BlockSpec((FFN_HALO, d), lambda i, j: (jnp.maximum(i * halo_blocks - 1, 0), 0)),
            pl.BlockSpec((1, d), lambda i, j: (0, 0)),
            pl.BlockSpec((d, tf), lambda i, j: (0, j)),
            pl.BlockSpec((d, tf), lambda i, j: (0, j + nf)),
            pl.BlockSpec((CONV_W, tf), lambda i, j: (0, j)),
            pl.BlockSpec((CONV_W, tf), lambda i, j: (0, j + nf)),
            pl.BlockSpec((1, tf), lambda i, j: (0, j)),
            pl.BlockSpec((1, tf), lambda i, j: (0, j + nf)),
            pl.BlockSpec((tf, d), lambda i, j: (j, 0)),
            pl.BlockSpec((1, d), lambda i, j: (0, 0)),
        ],
        out_specs=pl.BlockSpec((tm, d), lambda i, j: (i, 0)),
        out_shape=jax.ShapeDtypeStruct((m, d), F32),
        scratch_shapes=[
            pltpu.VMEM((tm + FFN_HALO, d), BF16),
            pltpu.VMEM((tm + FFN_HALO, tf), F32),
            pltpu.VMEM((tm + FFN_HALO, tf), F32),
            pltpu.VMEM((tm, d), F32),
        ],
        compiler_params=pltpu.CompilerParams(
            dimension_semantics=("parallel", "arbitrary"), vmem_limit_bytes=VMEM_LIMIT),
        name="conv_ffn",
    )(h, h, g_pre, w_up, w_up, conv_w, conv_w, conv_b, conv_b, w_down, g_post)


def _outproj_kernel(ya_ref, yb_ref, w_ref, h_ref, g_ref, o_ref):
    ka = ya_ref.shape[1]
    mix = _dot(ya_ref[...], w_ref[:ka, :]) + _dot(yb_ref[...], w_ref[ka:, :])
    o_ref[...] = h_ref[...] + _rms(mix, g_ref[...])


def _outproj(ya, yb, w, h, g, *, tm=512):
    m, d = h.shape
    return pl.pallas_call(
        _outproj_kernel,
        grid=(m // tm,),
        in_specs=[
            pl.BlockSpec((tm, ya.shape[1]), lambda i: (i, 0)),
            pl.BlockSpec((tm, yb.shape[1]), lambda i: (i, 0)),
            pl.BlockSpec(w.shape, lambda i: (0, 0)),
            pl.BlockSpec((tm, d), lambda i: (i, 0)),
            pl.BlockSpec((1, d), lambda i: (0, 0)),
        ],
        out_specs=pl.BlockSpec((tm, d), lambda i: (i, 0)),
        out_shape=jax.ShapeDtypeStruct((m, d), F32),
        compiler_params=pltpu.CompilerParams(
            dimension_semantics=("parallel",), vmem_limit_bytes=VMEM_LIMIT),
        name="outproj",
    )(ya, yb, w, h, g)


def _sgu_kernel(u_ref, v_ref, lng_ref, lnb_ref, ws_ref, bs_ref, wo_ref, h_ref, g_ref, o_ref,
                vln_ref, gated_ref, *, tm):
    v = v_ref[...]
    mu = jnp.mean(v, axis=-1, keepdims=True)
    dv = v - mu
    var = jnp.mean(dv * dv, axis=-1, keepdims=True)
    vln_ref[...] = (dv * lax.rsqrt(var + 1e-5) * lng_ref[...] + lnb_ref[...]).astype(BF16)

    gw = v.shape[1] // SG_GROUPS
    ii = lax.broadcasted_iota(jnp.int32, (SG_CHUNK, SG_CHUNK), 0)
    jj = lax.broadcasted_iota(jnp.int32, (SG_CHUNK, SG_CHUNK), 1)
    visible = (jj // CHUNK) <= (ii // CHUNK)
    for g in range(SG_GROUPS):
        wm = jnp.where(visible, ws_ref[g], 0.0).astype(BF16)
        bcol = bs_ref[:, g:g + 1]
        cols = slice(g * gw, (g + 1) * gw)
        for c in range(tm // SG_CHUNK):
            rows = slice(c * SG_CHUNK, (c + 1) * SG_CHUNK)
            s = _dot(wm, vln_ref[rows, cols]) + bcol
            gated_ref[rows, cols] = (u_ref[rows, cols] * s).astype(BF16)

    mix = _dot(gated_ref[...], wo_ref[...])
    o_ref[...] = h_ref[...] + _rms(mix, g_ref[...])


def _sgu(z, ln_g, ln_b, w_s, b_s_t, w_out, h, g, *, tm=256):
    m, d = h.shape
    return pl.pallas_call(
        functools.partial(_sgu_kernel, tm=tm),
        grid=(m // tm,),
        in_specs=[
            pl.BlockSpec((tm, d), lambda i: (i, 0)),
            pl.BlockSpec((tm, d), lambda i: (i, 1)),
            pl.BlockSpec((1, d), lambda i: (0, 0)),
            pl.BlockSpec((1, d), lambda i: (0, 0)),
            pl.BlockSpec(w_s.shape, lambda i: (0, 0, 0)),
            pl.BlockSpec(b_s_t.shape, lambda i: (0, 0)),
            pl.BlockSpec(w_out.shape, lambda i: (0, 0)),
            pl.BlockSpec((tm, d), lambda i: (i, 0)),
            pl.BlockSpec((1, d), lambda i: (0, 0)),
        ],
        out_specs=pl.BlockSpec((tm, d), lambda i: (i, 0)),
        out_shape=jax.ShapeDtypeStruct((m, d), F32),
        scratch_shapes=[pltpu.VMEM((tm, d), BF16), pltpu.VMEM((tm, d), BF16)],
        compiler_params=pltpu.CompilerParams(
            dimension_semantics=("parallel",), vmem_limit_bytes=VMEM_LIMIT),
        name="sgu",
    )(z, z, ln_g, ln_b, w_s, b_s_t, w_out, h, g)


RW_PP_ROWS = 16
(PP_MU_R, PP_MU_K, PP_MU_V, PP_W0, PP_A0, PP_KK, PP_KA, PP_RK, PP_GNG, PP_GNB) = range(10)
RW_CARRY = 8
RW_GROUP = 8


def _rwkv_kernel(r_ref, k_ref, v_ref, xx_ref, xg_ref, pp_ref, mux_ref, mug_ref, wup_ref, aup_ref,
                 gup_ref, y_ref,
                 rb_ref, kb_ref, vb_ref, xb_ref, gb_ref,
                 r_s, k_s, v_s, lw_s, kk_s, be_s, g_s, bonus_s, y_s, ra_s, z_s, phi_s, psi_s, state_ref,
                 *, tt):
    L = RW_CHUNK
    tb = pl.program_id(2)
    pp = pp_ref[...]
    row = lambda i: pp[i:i + 1, :]

    lane = lax.broadcasted_iota(jnp.int32, (1, LANES), 1)
    head_masks = (lane < A_HEAD_DIM, lane >= A_HEAD_DIM)
    li = lax.broadcasted_iota(jnp.int32, (LANES, LANES), 0)
    lj = lax.broadcasted_iota(jnp.int32, (LANES, LANES), 1)
    same_head = (li // A_HEAD_DIM) == (lj // A_HEAD_DIM)
    ones_bd = jnp.where(same_head, 1.0, 0.0).astype(BF16)

    def seg_sum(x):
        s0 = jnp.sum(jnp.where(head_masks[0], x, 0.0), axis=-1, keepdims=True)
        s1 = jnp.sum(jnp.where(head_masks[1], x, 0.0), axis=-1, keepdims=True)
        return jnp.where(head_masks[0], s0, s1)

    @pl.when(tb == 0)
    def _():
        for b in (rb_ref, kb_ref, vb_ref, xb_ref, gb_ref):
            b[0:RW_CARRY, :] = jnp.zeros((RW_CARRY, b.shape[1]), F32)
        state_ref[...] = jnp.zeros_like(state_ref)

    def shift_mix(x_ref, buf_ref, mu):
        x = x_ref[...]
        buf_ref[RW_CARRY:, :] = x
        prev = buf_ref[pl.ds(RW_CARRY - 1, tt), :]
        return x + (prev - x) * mu

    r = shift_mix(r_ref, rb_ref, row(PP_MU_R))
    k = shift_mix(k_ref, kb_ref, row(PP_MU_K))
    v = shift_mix(v_ref, vb_ref, row(PP_MU_V))
    xx = shift_mix(xx_ref, xb_ref, mux_ref[...])
    xg = shift_mix(xg_ref, gb_ref, mug_ref[...])
    for src, buf in ((r_ref, rb_ref), (k_ref, kb_ref), (v_ref, vb_ref), (xx_ref, xb_ref),
                     (xg_ref, gb_ref)):
        buf[0:RW_CARRY, :] = src[tt - RW_CARRY:tt, :]

    w_pre = row(PP_W0) + _dot(jnp.tanh(xx).astype(BF16), wup_ref[...])
    log_w = -(jnp.maximum(-w_pre, 0.0) + jnp.log1p(jnp.exp(-jnp.abs(w_pre)))) - 0.5
    lw_s[...] = -jnp.exp(log_w)
    a = jax.nn.sigmoid(row(PP_A0) + _dot(xx.astype(BF16), aup_ref[...]))
    g_s[...] = _dot(jax.nn.sigmoid(xg).astype(BF16), gup_ref[...])
    kk = k * row(PP_KK)
    kk = kk * lax.rsqrt(seg_sum(kk * kk) + 1e-12)
    k = k * (1.0 + (a - 1.0) * row(PP_KA))
    bonus_s[...] = seg_sum(r * k * row(PP_RK)) * v
    r_s[...] = r
    k_s[...] = k
    v_s[...] = v
    kk_s[...] = kk
    be_s[...] = kk * a

    tl = li % L
    cl = lj % L
    strict_bd = same_head & (cl < tl)
    incl_bd = same_head & (cl <= tl)
    tri_bd = jnp.where(incl_bd, 1.0, 0.0).astype(BF16)
    eye = jnp.where(li == lj, 1.0, 0.0).astype(F32)

    def stack(x):
        return jnp.concatenate([jnp.where(head_masks[0], x, 0.0), jnp.where(head_masks[1], x, 0.0)],
                               axis=0)

    state = [state_ref[...]]
    pending = []

    def tick():
        if pending:
            pending.pop(0)()

    def make_step(c):
        def step():
            s_hi, s_lo = _split2(state[0])
            y_st = _dot_nt(ra_s[c].astype(BF16), s_hi) + z_s[c]
            y_s[c * L:(c + 1) * L, :] = y_st[:L, :] + y_st[L:, :]
            p_hi, p_lo = _split2(phi_s[c])
            state[0] = _dot(s_hi, p_hi) + _dot(s_lo, p_hi) + _dot(s_hi, p_lo) + psi_s[c]
        return step

    def chunk_group(gi):
        G = range(RW_GROUP)
        cs, lw = [], []
        for half in range(RW_GROUP // 2):
            c0 = gi * RW_GROUP + 2 * half
            lw2 = lw_s[c0 * L:(c0 + 2) * L, :]
            h3 = _split3(lw2)
            cs2 = _dot(tri_bd, h3[0]) + _dot(tri_bd, h3[1]) + _dot(tri_bd, h3[2])
            cs += [cs2[:L, :], cs2[L:, :]]
            lw += [lw2[:L, :], lw2[L:, :]]
        tick()
        cidx = [gi * RW_GROUP + i for i in G]
        sls = [slice(c * L, (c + 1) * L) for c in cidx]
        e_pos = [jnp.exp(cs[i]) for i in G]
        e_prev = [jnp.exp(cs[i] - lw[i]) for i in G]
        e_neg = [jnp.exp(-cs[i]) for i in G]
        lam_l = [e_pos[i][L - 1:L, :] for i in G]
        rb_f = [stack(r_s[sls[i], :] * e_pos[i]) for i in G]
        ab = [stack(-kk_s[sls[i], :] * e_prev[i]).astype(BF16) for i in G]
        rb = [rb_f[i].astype(BF16) for i in G]
        bt = [stack(be_s[sls[i], :] * e_neg[i]).astype(BF16) for i in G]
        kt = [stack(k_s[sls[i], :] * e_neg[i]).astype(BF16) for i in G]
        v16 = [stack(v_s[sls[i], :]).astype(BF16) for i in G]
        tick()
        g = [_dot_nt(jnp.concatenate([ab[i], rb[i]], axis=0), jnp.concatenate([bt[i], kt[i]], axis=0))
             for i in G]
        a_ab = [jnp.where(strict_bd, g[i][:LANES, :LANES], 0.0) for i in G]
        a_ak = [jnp.where(strict_bd, g[i][:LANES, LANES:], 0.0).astype(BF16) for i in G]
        a_rb = [jnp.where(incl_bd, g[i][LANES:, :LANES], 0.0).astype(BF16) for i in G]
        a_rk = [jnp.where(incl_bd, g[i][LANES:, LANES:], 0.0).astype(BF16) for i in G]
        tick()
        t = [eye + a_ab[i] for i in G]
        p = a_ab
        for _ in range(5):
            p16 = [p[i].astype(BF16) for i in G]
            p = [_dot(p16[i], p16[i]) for i in G]
            tick()
            t = [t[i] + _dot(t[i].astype(BF16), p[i].astype(BF16)) for i in G]
            tick()
        t16 = [t[i].astype(BF16) for i in G]
        av = [_dot(a_ak[i], v16[i]).astype(BF16) for i in G]
        ta = [_dot(t16[i], ab[i]).astype(BF16) for i in G]
        tick()
        w16 = [_dot(t16[i], av[i]).astype(BF16) for i in G]
        for i in G:
            ra_s[cidx[i]] = rb_f[i] + _dot(a_rb[i], ta[i])
        tick()
        for i in G:
            z_s[cidx[i]] = _dot(a_rb[i], w16[i]) + _dot(a_rk[i], v16[i])
        tick()
        for i in G:
            phi_s[cidx[i]] = (eye + _dot_tn(ta[i], bt[i])) * lam_l[i]
        tick()
        for i in G:
            psi_s[cidx[i]] = (_dot_tn(w16[i], bt[i]) + _dot_tn(v16[i], kt[i])) * lam_l[i]
        while pending:
            tick()
        pending.extend(make_step(c) for c in cidx)

    for gi in range(tt // (L * RW_GROUP)):
        chunk_group(gi)
    while pending:
        tick()
    state_ref[...] = state[0]

    y = y_s[...]
    inv_n = 1.0 / A_HEAD_DIM
    mean = seg_sum(y) * inv_n
    d = y - mean
    var = seg_sum(d * d) * inv_n
    yn = d * lax.rsqrt(var + A_GN_EPS) * row(PP_GNG) + row(PP_GNB)
    y_ref[...] = ((yn + bonus_s[...]) * g_s[...]).astype(y_ref.dtype)


def _rwkv(z3, pp, mu_xx, mu_xg, wup, aup, gup, *, tt=1024):
    bsz, seq, _ = z3.shape
    pairs = A_WIDTH // LANES
    cb = lambda col, w: col // w
    col_spec = lambda col0: pl.BlockSpec((None, tt, LANES), lambda b, p, t: (b, t, cb(col0, LANES) + p))
    buf = lambda w: pltpu.VMEM((tt + RW_CARRY, w), F32)
    big = pltpu.VMEM((tt, LANES), F32)
    return pl.pallas_call(
        functools.partial(_rwkv_kernel, tt=tt),
        grid=(bsz, pairs, seq // tt),
        in_specs=[
            col_spec(COL_R), col_spec(COL_K), col_spec(COL_V),
            pl.BlockSpec((None, tt, LANES), lambda b, p, t: (b, t, cb(COL_XWXA, LANES))),
            pl.BlockSpec((None, tt, 256), lambda b, p, t: (b, t, cb(COL_XG, 256))),
            pl.BlockSpec((RW_PP_ROWS, LANES), lambda b, p, t: (0, p)),
            pl.BlockSpec((1, LANES), lambda b, p, t: (0, 0)),
            pl.BlockSpec((1, 256), lambda b, p, t: (0, 0)),
            pl.BlockSpec((LANES, LANES), lambda b, p, t: (0, p)),
            pl.BlockSpec((LANES, LANES), lambda b, p, t: (0, p)),
            pl.BlockSpec((256, LANES), lambda b, p, t: (0, p)),
        ],
        out_specs=pl.BlockSpec((None, tt, LANES), lambda b, p, t: (b, t, p)),
        out_shape=jax.ShapeDtypeStruct((bsz, seq, A_WIDTH), BF16),
        scratch_shapes=[buf(LANES), buf(LANES), buf(LANES), buf(LANES), buf(256)]
        + [big] * 9 + [pltpu.VMEM((tt // RW_CHUNK, LANES, LANES), F32)] * 4
        + [pltpu.VMEM((LANES, LANES), F32)],
        compiler_params=pltpu.CompilerParams(
            dimension_semantics=("parallel", "parallel", "arbitrary"), vmem_limit_bytes=VMEM_LIMIT),
        name="rwkv7",
    )(z3, z3, z3, z3, z3, pp, mu_xx, mu_xg, wup, aup, gup)


DSA_TK = 512


def _dsa_kernel(q_ref, qi_ref, kwq_ref, ckv_ref, kwf_ref, kvn_ref, ikn_ref, wuk_ref, wuv_ref,
                o_ref, ckv_s, ke_s, ko_s, key_s, qlat_s, m_s, l_s, acc_s, *, seq):
    j = pl.program_id(1)
    tk = DSA_TK
    qb = Q_BLOCK
    lane = lax.broadcasted_iota(jnp.int32, (1, LANES), 1)

    @pl.when(j == 0)
    def _():
        def prep(t, carry):
            sl = pl.ds(pl.multiple_of(t * tk, tk), tk)
            ckv_s[sl, :] = _rms(ckv_ref[sl, :], kvn_ref[...]).astype(BF16)
            kw = kwf_ref[sl, :]
            ksq = jnp.where(lane < IDX_HEAD_DIM, kw * kw, 0.0)
            ms = jnp.sum(ksq, axis=-1, keepdims=True) * (1.0 / IDX_HEAD_DIM)
            kn = kw * lax.rsqrt(ms + NORM_EPS) * ikn_ref[...]
            ke_s[sl, :] = kn.astype(BF16)
            ko_s[sl, :] = pltpu.roll(kn, IDX_HEAD_DIM, axis=1).astype(BF16)
            return carry
        lax.fori_loop(0, seq // tk, prep, 0)

    n_tiles = (j * qb + qb + tk - 1) // tk
    qrow = lax.broadcasted_iota(jnp.int32, (qb, 1), 0)
    limit = j * qb + (qrow // CHUNK + 1) * CHUNK

    w_all = kwq_ref[...] * ((IDX_HEADS ** -0.5) * (IDX_HEAD_DIM ** -0.5))
    qi16 = [qi_ref[:, p * LANES:(p + 1) * LANES].astype(BF16) for p in range(IDX_HEADS // 2)]
    w_cols = [w_all[:, IDX_HEAD_DIM + h:IDX_HEAD_DIM + h + 1] for h in range(IDX_HEADS)]

    def score_tile(t, carry):
        sl = pl.ds(pl.multiple_of(t * tk, tk), tk)
        ke = ke_s[sl, :]
        ko = ko_s[sl, :]
        sc = jnp.zeros((qb, tk), F32)
        for p in range(IDX_HEADS // 2):
            sc = sc + jnp.maximum(_dot_nt(qi16[p], ke), 0.0) * w_cols[2 * p]
            sc = sc + jnp.maximum(_dot_nt(qi16[p], ko), 0.0) * w_cols[2 * p + 1]
        kpos = t * tk + lax.broadcasted_iota(jnp.int32, (1, tk), 1)
        sc = jnp.where(kpos < limit, sc, -jnp.inf)
        bits = pltpu.bitcast(sc, jnp.int32)
        key_s[t] = bits ^ ((bits >> 31) & 0x7FFFFFFF)
        return carry

    lax.fori_loop(0, n_tiles, score_tile, 0)

    def count_ge(cand):
        def body(t, cnt):
            ge = jnp.where(key_s[t] >= cand, 1.0, 0.0)
            return cnt + jnp.sum(ge, axis=-1, keepdims=True)
        return lax.fori_loop(0, n_tiles, body, jnp.zeros((qb, 1), F32))

    int_min = jnp.int32(-2 ** 31)
    topk = float(TOPK_MAX)
    theta = jnp.where(count_ge(jnp.zeros((qb, 1), jnp.int32)) >= topk, 0, int_min)

    def bit_step(i, theta):
        cand = theta | (jnp.int32(1) << (30 - i))
        return jnp.where(count_ge(cand) >= topk, cand, theta)

    theta = lax.fori_loop(0, 31, bit_step, theta)

    for h in range(B_HEADS):
        qh = q_ref[:, h * B_HEAD_DIM:(h + 1) * B_HEAD_DIM].astype(BF16)
        qlat_s[h * qb:(h + 1) * qb, :] = (_dot(qh, wuk_ref[h]) * (B_HEAD_DIM ** -0.5)).astype(BF16)

    m_s[...] = jnp.full_like(m_s, NEG_BIG)
    l_s[...] = jnp.zeros_like(l_s)
    acc_s[...] = jnp.zeros_like(acc_s)

    def attend(t, carry):
        sl = pl.ds(pl.multiple_of(t * tk, tk), tk)
        ckv = ckv_s[sl, :]
        kpos = t * tk + lax.broadcasted_iota(jnp.int32, (1, tk), 1)
        sel = (key_s[t] >= theta) & (kpos < limit)
        logits = _dot_nt(qlat_s[...], ckv).reshape(B_HEADS, qb, tk)
        logits = jnp.where(sel[None, :, :], logits, NEG_BIG)
        m_old = m_s[...]
        m_new = jnp.maximum(m_old, jnp.max(logits, axis=-1, keepdims=True))
        alpha = jnp.exp(m_old - m_new)
        p = jnp.exp(logits - m_new)
        l_s[...] = alpha * l_s[...] + jnp.sum(p, axis=-1, keepdims=True)
        pv = _dot(p.reshape(B_HEADS * qb, tk).astype(BF16), ckv).reshape(B_HEADS, qb, KV_LATENT)
        acc_s[...] = alpha * acc_s[...] + pv
        m_s[...] = m_new
        return carry

    lax.fori_loop(0, n_tiles, attend, 0)

    o_lat = acc_s[...] / l_s[...]
    for h in range(B_HEADS):
        o_ref[:, h * B_HEAD_DIM:(h + 1) * B_HEAD_DIM] = _dot(
            o_lat[h].astype(BF16), wuv_ref[h]).astype(o_ref.dtype)


def _dsa(z3, kv_norm, ikn_pad, w_uk, w_uv):
    bsz, seq, _ = z3.shape
    qb = Q_BLOCK
    return pl.pallas_call(
        functools.partial(_dsa_kernel, seq=seq),
        grid=(bsz, seq // qb),
        in_specs=[
            pl.BlockSpec((None, qb, B_WIDTH), lambda b, j: (b, j, COL_Q // B_WIDTH)),
            pl.BlockSpec((None, qb, B_WIDTH), lambda b, j: (b, j, COL_QI // B_WIDTH)),
            pl.BlockSpec((None, qb, LANES), lambda b, j: (b, j, COL_KW // LANES)),
            pl.BlockSpec((None, seq, KV_LATENT), lambda b, j: (b, 0, COL_CKV // KV_LATENT)),
            pl.BlockSpec((None, seq, LANES), lambda b, j: (b, 0, COL_KW // LANES)),
            pl.BlockSpec((1, KV_LATENT), lambda b, j: (0, 0)),
            pl.BlockSpec((1, LANES), lambda b, j: (0, 0)),
            pl.BlockSpec(w_uk.shape, lambda b, j: (0, 0, 0)),
            pl.BlockSpec(w_uv.shape, lambda b, j: (0, 0, 0)),
        ],
        out_specs=pl.BlockSpec((None, qb, B_WIDTH), lambda b, j: (b, j, 0)),
        out_shape=jax.ShapeDtypeStruct((bsz, seq, B_WIDTH), BF16),
        scratch_shapes=[
            pltpu.VMEM((seq, KV_LATENT), BF16),
            pltpu.VMEM((seq, LANES), BF16),
            pltpu.VMEM((seq, LANES), BF16),
            pltpu.VMEM((seq // DSA_TK, qb, DSA_TK), jnp.int32),
            pltpu.VMEM((B_HEADS * qb, KV_LATENT), BF16),
            pltpu.VMEM((B_HEADS, qb, 1), F32),
            pltpu.VMEM((B_HEADS, qb, 1), F32),
            pltpu.VMEM((B_HEADS, qb, KV_LATENT), F32),
        ],
        compiler_params=pltpu.CompilerParams(
            dimension_semantics=("parallel", "arbitrary"), vmem_limit_bytes=VMEM_LIMIT),
        name="dsa",
    )(z3, z3, z3, z3, z3, kv_norm, ikn_pad, w_uk, w_uv)


def _pad_cols(w, width):
    return jnp.pad(w, ((0, 0), (0, width - w.shape[1])))


def _layer0_params(ab_w_in, mu, w0, w_up, a0, a_up, g_up, k_k, k_a, r_k, gn_g, gn_b, idx_k_norm):
    a_in = 3 * A_WIDTH + LORA_W + LORA_A + LORA_G
    o = 3 * A_WIDTH
    w_r, w_k, w_v = ab_w_in[:, :A_WIDTH], ab_w_in[:, A_WIDTH:2 * A_WIDTH], ab_w_in[:, 2 * A_WIDTH:o]
    w_xwxa = ab_w_in[:, o:o + LORA_W + LORA_A]
    w_xg = ab_w_in[:, o + LORA_W + LORA_A:a_in]
    wb = ab_w_in[:, a_in:]
    o1 = B_WIDTH
    o2 = o1 + KV_LATENT
    o3 = o2 + IDX_HEADS * IDX_HEAD_DIM
    w_q, w_ckv, w_qi, w_kw = wb[:, :o1], wb[:, o1:o2], wb[:, o2:o3], wb[:, o3:]
    w_in = jnp.concatenate(
        [w_r, w_k, w_v, w_q, w_qi, w_ckv, _pad_cols(w_xg, 256), w_xwxa, _pad_cols(w_kw, LANES)], axis=1)
    w_in = _pad_cols(w_in, AB_IN_PAD).astype(BF16)

    rows = [mu[:A_WIDTH], mu[A_WIDTH:2 * A_WIDTH], mu[2 * A_WIDTH:o], w0, a0, k_k, k_a,
            r_k.reshape(-1), gn_g, gn_b]
    pp = jnp.stack(rows + [jnp.zeros_like(w0)] * (RW_PP_ROWS - len(rows)), axis=0)
    mu_xx = mu[o:o + LORA_W + LORA_A][None, :]
    mu_xg = jnp.pad(mu[o + LORA_W + LORA_A:a_in], (0, 256 - LORA_G))[None, :]
    wup = jnp.pad(w_up, ((0, LANES - LORA_W), (0, 0))).astype(BF16)
    aup = jnp.pad(a_up, ((LORA_W, 0), (0, 0))).astype(BF16)
    gup = jnp.pad(g_up, ((0, 256 - LORA_G), (0, 0))).astype(BF16)
    ikn = jnp.pad(idx_k_norm, (0, LANES - IDX_HEAD_DIM))[None, :]
    return w_in, pp, mu_xx, mu_xg, wup, aup, gup, ikn


def kernel(x, norm_g, ab_w_in, ab_shift_mu, rw_w0, rw_w_up, rw_a0, rw_a_up, rw_g_up, rw_k_k, rw_k_a, rw_r_k, rw_gn_g, rw_gn_b, mla_kv_norm, mla_w_uk, mla_w_uv, idx_k_norm, ab_w_out, sg_w_in, sg_b_in, sg_ln_g, sg_ln_b, sg_w_s, sg_b_s, sg_w_out, ffn_w_up, ffn_conv_w, ffn_conv_b, ffn_w_down):
    bsz, seq, d = x.shape
    n = bsz * seq
    depth = norm_g.shape[0]
    h = x.reshape(n, d)
    for layer in range(depth):
        g_pre_mix, g_post_mix, g_pre_ffn, g_post_ffn = [norm_g[layer, i][None, :] for i in range(4)]
        if layer % 2 == 0:
            e = layer // 2
            w_in, pp, mu_xx, mu_xg, wup, aup, gup, ikn = _layer0_params(
                ab_w_in[e], ab_shift_mu[e], rw_w0[e], rw_w_up[e], rw_a0[e], rw_a_up[e], rw_g_up[e],
                rw_k_k[e], rw_k_a[e], rw_r_k[e], rw_gn_g[e], rw_gn_b[e], idx_k_norm[e])
            z = _norm_matmul(h, g_pre_mix, w_in, jnp.zeros((1, AB_IN_PAD), F32), gelu=False)
            z3 = z.reshape(bsz, seq, AB_IN_PAD)
            y_a = _rwkv(z3, pp, mu_xx, mu_xg, wup, aup, gup)
            y_b = _dsa(z3, mla_kv_norm[e][None, :], ikn, mla_w_uk[e].astype(BF16),
                       mla_w_uv[e].astype(BF16))
            h = _outproj(y_a.reshape(n, A_WIDTH), y_b.reshape(n, B_WIDTH),
                         ab_w_out[e].astype(BF16), h, g_post_mix)
        else:
            o = layer // 2
            z = _norm_matmul(h, g_pre_mix, sg_w_in[o].astype(BF16), sg_b_in[o][None, :], gelu=True)
            h = _sgu(z, sg_ln_g[o][None, :], sg_ln_b[o][None, :], sg_w_s[o], sg_b_s[o].T,
                     sg_w_out[o].astype(BF16), h, g_post_mix)
        h = _ffn(h, g_pre_ffn, g_post_ffn, ffn_w_up[layer].astype(BF16), ffn_conv_w[layer],
                 ffn_conv_b[layer][None, :], ffn_w_down[layer].astype(BF16), seq=seq)
    return h.reshape(bsz, seq, d)
```

```python
import functools

import jax
import jax.numpy as jnp
from jax import lax
from jax.experimental import pallas as pl
from jax.experimental.pallas import tpu as pltpu

F32 = jnp.float32
BF16 = jnp.bfloat16

NORM_EPS = 1e-6
CHUNK = 64
A_HEADS = 16
A_HEAD_DIM = 64
A_WIDTH = A_HEADS * A_HEAD_DIM
LORA_W = 64
LORA_A = 64
LORA_G = 160
A_GN_EPS = 64e-5
B_HEADS = 8
B_HEAD_DIM = 128
B_WIDTH = B_HEADS * B_HEAD_DIM
KV_LATENT = 256
IDX_HEADS = 16
IDX_HEAD_DIM = 64
TOPK_MAX = 256
Q_BLOCK = 128
SG_CHUNK = 128
SG_GROUPS = 8
CONV_W = 3

LANES = 128
BF16_SUBLANES = 16
VMEM_LIMIT = 56 * 1024 * 1024

COL_R = 0
COL_K = A_WIDTH
COL_V = 2 * A_WIDTH
COL_Q = 3 * A_WIDTH
COL_QI = COL_Q + B_WIDTH
COL_CKV = COL_QI + IDX_HEADS * IDX_HEAD_DIM
COL_XG = COL_CKV + KV_LATENT
COL_XWXA = COL_XG + 256
COL_KW = COL_XWXA + LANES
AB_IN_PAD = 6144

RW_CHUNK = 64
NEG_BIG = -1e30


def _rms(x, g):
    ms = jnp.mean(x * x, axis=-1, keepdims=True)
    return x * lax.rsqrt(ms + NORM_EPS) * g


def _dot(a, b):
    return jnp.dot(a, b, preferred_element_type=F32)


def _dot_nt(a, b):
    return lax.dot_general(a, b, (((1,), (1,)), ((), ())), preferred_element_type=F32)


def _dot_tn(a, b):
    return lax.dot_general(a, b, (((0,), (0,)), ((), ())), preferred_element_type=F32)


def _split2(x):
    hi = x.astype(BF16)
    lo = (x - hi.astype(F32)).astype(BF16)
    return hi, lo


def _split3(x):
    hi = x.astype(BF16)
    r1 = x - hi.astype(F32)
    mid = r1.astype(BF16)
    lo = (r1 - mid.astype(F32)).astype(BF16)
    return hi, mid, lo


def _norm_mm_kernel(x_ref, g_ref, w_ref, b_ref, o_ref, xn_ref, *, gelu):
    @pl.when(pl.program_id(1) == 0)
    def _():
        xn_ref[...] = _rms(x_ref[...], g_ref[...]).astype(BF16)

    y = _dot(xn_ref[...], w_ref[...])
    if gelu:
        y = y + b_ref[...]
        y = 0.5 * y * (1.0 + lax.erf(y * 0.7071067811865476))
    o_ref[...] = y.astype(o_ref.dtype)


def _norm_matmul(x, g, w, b, *, gelu, tm=1024, tn=1024, out_dtype=F32):
    m, k = x.shape
    n = w.shape[1]
    return pl.pallas_call(
        functools.partial(_norm_mm_kernel, gelu=gelu),
        grid=(m // tm, n // tn),
        in_specs=[
            pl.BlockSpec((tm, k), lambda i, j: (i, 0)),
            pl.BlockSpec((1, k), lambda i, j: (0, 0)),
            pl.BlockSpec((k, tn), lambda i, j: (0, j)),
            pl.BlockSpec((1, tn), lambda i, j: (0, j)),
        ],
        out_specs=pl.BlockSpec((tm, tn), lambda i, j: (i, j)),
        out_shape=jax.ShapeDtypeStruct((m, n), out_dtype),
        scratch_shapes=[pltpu.VMEM((tm, k), BF16)],
        compiler_params=pltpu.CompilerParams(
            dimension_semantics=("parallel", "arbitrary"), vmem_limit_bytes=VMEM_LIMIT),
        name="norm_matmul",
    )(x, g, w, b)


FFN_HALO = BF16_SUBLANES


def _ffn_kernel(x_ref, xp_ref, gpre_ref, wg_ref, wu_ref, cwg_ref, cwu_ref, cbg_ref, cbu_ref,
                wd_ref, gpost_ref, o_ref, xn_ref, hg_ref, hu_ref, acc_ref, *, tm, tiles_per_seq):
    i = pl.program_id(0)
    j = pl.program_id(1)

    @pl.when(j == 0)
    def _():
        xn_ref[FFN_HALO:, :] = _rms(x_ref[...], gpre_ref[...]).astype(BF16)
        keep = jnp.where(i % tiles_per_seq != 0, 1.0, 0.0)
        xn_ref[:FFN_HALO, :] = (_rms(xp_ref[...], gpre_ref[...]) * keep).astype(BF16)
        acc_ref[...] = jnp.zeros_like(acc_ref)

    xn = xn_ref[...]
    hg_ref[...] = _dot(xn, wg_ref[...])
    hu_ref[...] = _dot(xn, wu_ref[...])

    def conv(h_ref, cw_ref, cb_ref):
        cw = cw_ref[...]
        return (h_ref[pl.ds(FFN_HALO, tm), :] * cw[2:3, :]
                + h_ref[pl.ds(FFN_HALO - 1, tm), :] * cw[1:2, :]
                + h_ref[pl.ds(FFN_HALO - 2, tm), :] * cw[0:1, :]
                + cb_ref[...])

    gate = conv(hg_ref, cwg_ref, cbg_ref)
    up = conv(hu_ref, cwu_ref, cbu_ref)
    act = gate * jax.nn.sigmoid(gate) * up
    acc_ref[...] += _dot(act.astype(BF16), wd_ref[...])

    @pl.when(j == pl.num_programs(1) - 1)
    def _():
        o_ref[...] = x_ref[...] + _rms(acc_ref[...], gpost_ref[...])


def _ffn(h, g_pre, g_post, w_up, conv_w, conv_b, w_down, *, seq, tm=512, tf=512):
    m, d = h.shape
    d_ff = w_down.shape[0]
    nf = d_ff // tf
    halo_blocks = tm // FFN_HALO
    return pl.pallas_call(
        functools.partial(_ffn_kernel, tm=tm, tiles_per_seq=seq // tm),
        grid=(m // tm, nf),
        in_specs=[
            pl.BlockSpec((tm, d), lambda i, j: (i, 0)),
            pl.BlockSpec((FFN_HALO, d), lambda i, j: (jnp.maximum(i * halo_blocks - 1, 0), 0)),
            pl.BlockSpec((1, d), lambda i, j: (0, 0)),
            pl.BlockSpec((d, tf), lambda i, j: (0, j)),
            pl.BlockSpec((d, tf), lambda i, j: (0, j + nf)),
            pl.BlockSpec((CONV_W, tf), lambda i, j: (0, j)),
            pl.BlockSpec((CONV_W, tf), lambda i, j: (0, j + nf)),
            pl.BlockSpec((1, tf), lambda i, j: (0, j)),
            pl.BlockSpec((1, tf), lambda i, j: (0, j + nf)),
            pl.BlockSpec((tf, d), lambda i, j: (j, 0)),
            pl.BlockSpec((1, d), lambda i, j: (0, 0)),
        ],
        out_specs=pl.BlockSpec((tm, d), lambda i, j: (i, 0)),
        out_shape=jax.ShapeDtypeStruct((m, d), F32),
        scratch_shapes=[
            pltpu.VMEM((tm + FFN_HALO, d), BF16),
            pltpu.VMEM((tm + FFN_HALO, tf), F32),
            pltpu.VMEM((tm + FFN_HALO, tf), F32),
            pltpu.VMEM((tm, d), F32),
        ],
        compiler_params=pltpu.CompilerParams(
            dimension_semantics=("parallel", "arbitrary"), vmem_limit_bytes=VMEM_LIMIT),
        name="conv_ffn",
    )(h, h, g_pre, w_up, w_up, conv_w, conv_w, conv_b, conv_b, w_down, g_post)


def _outproj_kernel(ya_ref, yb_ref, w_ref, h_ref, g_ref, o_ref):
    ka = ya_ref.shape[1]
    mix = _dot(ya_ref[...], w_ref[:ka, :]) + _dot(yb_ref[...], w_ref[ka:, :])
    o_ref[...] = h_ref[...] + _rms(mix, g_ref[...])


def _outproj(ya, yb, w, h, g, *, tm=512):
    m, d = h.shape
    return pl.pallas_call(
        _outproj_kernel,
        grid=(m // tm,),
        in_specs=[
            pl.BlockSpec((tm, ya.shape[1]), lambda i: (i, 0)),
            pl.BlockSpec((tm, yb.shape[1]), lambda i: (i, 0)),
            pl.BlockSpec(w.shape, lambda i: (0, 0)),
            pl.BlockSpec((tm, d), lambda i: (i, 0)),
            pl.BlockSpec((1, d), lambda i: (0, 0)),
        ],
        out_specs=pl.BlockSpec((tm, d), lambda i: (i, 0)),
        out_shape=jax.ShapeDtypeStruct((m, d), F32),
        compiler_params=pltpu.CompilerParams(
            dimension_semantics=("parallel",), vmem_limit_bytes=VMEM_LIMIT),
        name="outproj",
    )(ya, yb, w, h, g)


def _sgu_kernel(u_ref, v_ref, lng_ref, lnb_ref, ws_ref, bs_ref, wo_ref, h_ref, g_ref, o_ref,
                vln_ref, gated_ref, *, tm):
    v = v_ref[...]
    mu = jnp.mean(v, axis=-1, keepdims=True)
    dv = v - mu
    var = jnp.mean(dv * dv, axis=-1, keepdims=True)
    vln_ref[...] = (dv * lax.rsqrt(var + 1e-5) * lng_ref[...] + lnb_ref[...]).astype(BF16)

    gw = v.shape[1] // SG_GROUPS
    ii = lax.broadcasted_iota(jnp.int32, (SG_CHUNK, SG_CHUNK), 0)
    jj = lax.broadcasted_iota(jnp.int32, (SG_CHUNK, SG_CHUNK), 1)
    visible = (jj // CHUNK) <= (ii // CHUNK)
    for g in range(SG_GROUPS):
        wm = jnp.where(visible, ws_ref[g], 0.0).astype(BF16)
        bcol = bs_ref[:, g:g + 1]
        cols = slice(g * gw, (g + 1) * gw)
        for c in range(tm // SG_CHUNK):
            rows = slice(c * SG_CHUNK, (c + 1) * SG_CHUNK)
            s = _dot(wm, vln_ref[rows, cols]) + bcol
            gated_ref[rows, cols] = (u_ref[rows, cols] * s).astype(BF16)

    mix = _dot(gated_ref[...], wo_ref[...])
    o_ref[...] = h_ref[...] + _rms(mix, g_ref[...])


def _sgu(z, ln_g, ln_b, w_s, b_s_t, w_out, h, g, *, tm=256):
    m, d = h.shape
    return pl.pallas_call(
        functools.partial(_sgu_kernel, tm=tm),
        grid=(m // tm,),
        in_specs=[
            pl.BlockSpec((tm, d), lambda i: (i, 0)),
            pl.BlockSpec((tm, d), lambda i: (i, 1)),
            pl.BlockSpec((1, d), lambda i: (0, 0)),
            pl.BlockSpec((1, d), lambda i: (0, 0)),
            pl.BlockSpec(w_s.shape, lambda i: (0, 0, 0)),
            pl.BlockSpec(b_s_t.shape, lambda i: (0, 0)),
            pl.BlockSpec(w_out.shape, lambda i: (0, 0)),
            pl.BlockSpec((tm, d), lambda i: (i, 0)),
            pl.BlockSpec((1, d), lambda i: (0, 0)),
        ],
        out_specs=pl.BlockSpec((tm, d), lambda i: (i, 0)),
        out_shape=jax.ShapeDtypeStruct((m, d), F32),
        scratch_shapes=[pltpu.VMEM((tm, d), BF16), pltpu.VMEM((tm, d), BF16)],
        compiler_params=pltpu.CompilerParams(
            dimension_semantics=("parallel",), vmem_limit_bytes=VMEM_LIMIT),
        name="sgu",
    )(z, z, ln_g, ln_b, w_s, b_s_t, w_out, h, g)


RW_PP_ROWS = 16
(PP_MU_R, PP_MU_K, PP_MU_V, PP_W0, PP_A0, PP_KK, PP_KA, PP_RK, PP_GNG, PP_GNB) = range(10)
RW_CARRY = 8
RW_GROUP = 8


def _rwkv_kernel(r_ref, k_ref, v_ref, xx_ref, xg_ref, pp_ref, mux_ref, mug_ref, wup_ref, aup_ref,
                 gup_ref, y_ref,
                 rb_ref, kb_ref, vb_ref, xb_ref, gb_ref,
                 r_s, k_s, v_s, lw_s, kk_s, be_s, g_s, bonus_s, y_s, ra_s, z_s, phi_s, psi_s, state_ref,
                 *, tt):
    L = RW_CHUNK
    tb = pl.program_id(2)
    pp = pp_ref[...]
    row = lambda i: pp[i:i + 1, :]

    lane = lax.broadcasted_iota(jnp.int32, (1, LANES), 1)
    head_masks = (lane < A_HEAD_DIM, lane >= A_HEAD_DIM)
    li = lax.broadcasted_iota(jnp.int32, (LANES, LANES), 0)
    lj = lax.broadcasted_iota(jnp.int32, (LANES, LANES), 1)
    same_head = (li // A_HEAD_DIM) == (lj // A_HEAD_DIM)
    ones_bd = jnp.where(same_head, 1.0, 0.0).astype(BF16)

    def seg_sum(x):
        s0 = jnp.sum(jnp.where(head_masks[0], x, 0.0), axis=-1, keepdims=True)
        s1 = jnp.sum(jnp.where(head_masks[1], x, 0.0), axis=-1, keepdims=True)
        return jnp.where(head_masks[0], s0, s1)

    @pl.when(tb == 0)
    def _():
        for b in (rb_ref, kb_ref, vb_ref, xb_ref, gb_ref):
            b[0:RW_CARRY, :] = jnp.zeros((RW_CARRY, b.shape[1]), F32)
        state_ref[...] = jnp.zeros_like(state_ref)

    def shift_mix(x_ref, buf_ref, mu):
        x = x_ref[...]
        buf_ref[RW_CARRY:, :] = x
        prev = buf_ref[pl.ds(RW_CARRY - 1, tt), :]
        return x + (prev - x) * mu

    r = shift_mix(r_ref, rb_ref, row(PP_MU_R))
    k = shift_mix(k_ref, kb_ref, row(PP_MU_K))
    v = shift_mix(v_ref, vb_ref, row(PP_MU_V))
    xx = shift_mix(xx_ref, xb_ref, mux_ref[...])
    xg = shift_mix(xg_ref, gb_ref, mug_ref[...])
    for src, buf in ((r_ref, rb_ref), (k_ref, kb_ref), (v_ref, vb_ref), (xx_ref, xb_ref),
                     (xg_ref, gb_ref)):
        buf[0:RW_CARRY, :] = src[tt - RW_CARRY:tt, :]

    w_pre = row(PP_W0) + _dot(jnp.tanh(xx).astype(BF16), wup_ref[...])
    log_w = -(jnp.maximum(-w_pre, 0.0) + jnp.log1p(jnp.exp(-jnp.abs(w_pre)))) - 0.5
    lw_s[...] = -jnp.exp(log_w)
    a = jax.nn.sigmoid(row(PP_A0) + _dot(xx.astype(BF16), aup_ref[...]))
    g_s[...] = _dot(jax.nn.sigmoid(xg).astype(BF16), gup_ref[...])
    kk = k * row(PP_KK)
    kk = kk * lax.rsqrt(seg_sum(kk * kk) + 1e-12)
    k = k * (1.0 + (a - 1.0) * row(PP_KA))
    bonus_s[...] = seg_sum(r * k * row(PP_RK)) * v
    r_s[...] = r
    k_s[...] = k
    v_s[...] = v
    kk_s[...] = kk
    be_s[...] = kk * a

    tl = li % L
    cl = lj % L
    strict_bd = same_head & (cl < tl)
    incl_bd = same_head & (cl <= tl)
    tri_bd = jnp.where(incl_bd, 1.0, 0.0).astype(BF16)
    eye = jnp.where(li == lj, 1.0, 0.0).astype(F32)

    def stack(x):
        return jnp.concatenate([jnp.where(head_masks[0], x, 0.0), jnp.where(head_masks[1], x, 0.0)],
                               axis=0)

    state = [state_ref[...]]
    pending = []

    def tick():
        if pending:
            pending.pop(0)()

    def make_step(c):
        def step():
            s_hi, s_lo = _split2(state[0])
            y_st = _dot_nt(ra_s[c].astype(BF16), s_hi) + z_s[c]
            y_s[c * L:(c + 1) * L, :] = y_st[:L, :] + y_st[L:, :]
            p_hi, p_lo = _split2(phi_s[c])
            state[0] = _dot(s_hi, p_hi) + _dot(s_lo, p_hi) + _dot(s_hi, p_lo) + psi_s[c]
        return step

    def chunk_group(gi):
        G = range(RW_GROUP)
        cs, lw = [], []
        for half in range(RW_GROUP // 2):
            c0 = gi * RW_GROUP + 2 * half
            lw2 = lw_s[c0 * L:(c0 + 2) * L, :]
            h3 = _split3(lw2)
            cs2 = _dot(tri_bd, h3[0]) + _dot(tri_bd, h3[1]) + _dot(tri_bd, h3[2])
            cs += [cs2[:L, :], cs2[L:, :]]
            lw += [lw2[:L, :], lw2[L:, :]]
        tick()
        cidx = [gi * RW_GROUP + i for i in G]
        sls = [slice(c * L, (c + 1) * L) for c in cidx]
        e_pos = [jnp.exp(cs[i]) for i in G]
        e_prev = [jnp.exp(cs[i] - lw[i]) for i in G]
        e_neg = [jnp.exp(-cs[i]) for i in G]
        lam_l = [e_pos[i][L - 1:L, :] for i in G]
        rb_f = [stack(r_s[sls[i], :] * e_pos[i]) for i in G]
        ab = [stack(-kk_s[sls[i], :] * e_prev[i]).astype(BF16) for i in G]
        rb = [rb_f[i].astype(BF16) for i in G]
        bt = [stack(be_s[sls[i], :] * e_neg[i]).astype(BF16) for i in G]
        kt = [stack(k_s[sls[i], :] * e_neg[i]).astype(BF16) for i in G]
        v16 = [stack(v_s[sls[i], :]).astype(BF16) for i in G]
        tick()
        g = [_dot_nt(jnp.concatenate([ab[i], rb[i]], axis=0), jnp.concatenate([bt[i], kt[i]], axis=0))
             for i in G]
        a_ab = [jnp.where(strict_bd, g[i][:LANES, :LANES], 0.0) for i in G]
        a_ak = [jnp.where(strict_bd, g[i][:LANES, LANES:], 0.0).astype(BF16) for i in G]
        a_rb = [jnp.where(incl_bd, g[i][LANES:, :LANES], 0.0).astype(BF16) for i in G]
        a_rk = [jnp.where(incl_bd, g[i][LANES:, LANES:], 0.0).astype(BF16) for i in G]
        tick()
        t = [eye + a_ab[i] for i in G]
        p = a_ab
        for _ in range(5):
            p16 = [p[i].astype(BF16) for i in G]
            p = [_dot(p16[i], p16[i]) for i in G]
            tick()
            t = [t[i] + _dot(t[i].astype(BF16), p[i].astype(BF16)) for i in G]
            tick()
        t16 = [t[i].astype(BF16) for i in G]
        av = [_dot(a_ak[i], v16[i]).astype(BF16) for i in G]
        ta = [_dot(t16[i], ab[i]).astype(BF16) for i in G]
        tick()
        w16 = [_dot(t16[i], av[i]).astype(BF16) for i in G]
        for i in G:
            ra_s[cidx[i]] = rb_f[i] + _dot(a_rb[i], ta[i])
        tick()
        for i in G:
            z_s[cidx[i]] = _dot(a_rb[i], w16[i]) + _dot(a_rk[i], v16[i])
        tick()
        for i in G:
            phi_s[cidx[i]] = (eye + _dot_tn(ta[i], bt[i])) * lam_l[i]
        tick()
        for i in G:
            psi_s[cidx[i]] = (_dot_tn(w16[i], bt[i]) + _dot_tn(v16[i], kt[i])) * lam_l[i]
        while pending:
            tick()
        pending.extend(make_step(c) for c in cidx)

    for gi in range(tt // (L * RW_GROUP)):
        chunk_group(gi)
    while pending:
        tick()
    state_ref[...] = state[0]

    y = y_s[...]
    inv_n = 1.0 / A_HEAD_DIM
    mean = seg_sum(y) * inv_n
    d = y - mean
    var = seg_sum(d * d) * inv_n
    yn = d * lax.rsqrt(var + A_GN_EPS) * row(PP_GNG) + row(PP_GNB)
    y_ref[...] = ((yn + bonus_s[...]) * g_s[...]).astype(y_ref.dtype)


def _rwkv(z3, pp, mu_xx, mu_xg, wup, aup, gup, *, tt=1024):
    bsz, seq, _ = z3.shape
    pairs = A_WIDTH // LANES
    cb = lambda col, w: col // w
    col_spec = lambda col0: pl.BlockSpec((None, tt, LANES), lambda b, p, t: (b, t, cb(col0, LANES) + p))
    buf = lambda w: pltpu.VMEM((tt + RW_CARRY, w), F32)
    big = pltpu.VMEM((tt, LANES), F32)
    return pl.pallas_call(
        functools.partial(_rwkv_kernel, tt=tt),
        grid=(bsz, pairs, seq // tt),
        in_specs=[
            col_spec(COL_R), col_spec(COL_K), col_spec(COL_V),
            pl.BlockSpec((None, tt, LANES), lambda b, p, t: (b, t, cb(COL_XWXA, LANES))),
            pl.BlockSpec((None, tt, 256), lambda b, p, t: (b, t, cb(COL_XG, 256))),
            pl.BlockSpec((RW_PP_ROWS, LANES), lambda b, p, t: (0, p)),
            pl.BlockSpec((1, LANES), lambda b, p, t: (0, 0)),
            pl.BlockSpec((1, 256), lambda b, p, t: (0, 0)),
            pl.BlockSpec((LANES, LANES), lambda b, p, t: (0, p)),
            pl.BlockSpec((LANES, LANES), lambda b, p, t: (0, p)),
            pl.BlockSpec((256, LANES), lambda b, p, t: (0, p)),
        ],
        out_specs=pl.BlockSpec((None, tt, LANES), lambda b, p, t: (b, t, p)),
        out_shape=jax.ShapeDtypeStruct((bsz, seq, A_WIDTH), BF16),
        scratch_shapes=[buf(LANES), buf(LANES), buf(LANES), buf(LANES), buf(256)]
        + [big] * 9 + [pltpu.VMEM((tt // RW_CHUNK, LANES, LANES), F32)] * 4
        + [pltpu.VMEM((LANES, LANES), F32)],
        compiler_params=pltpu.CompilerParams(
            dimension_semantics=("parallel", "parallel", "arbitrary"), vmem_limit_bytes=VMEM_LIMIT),
        name="rwkv7",
    )(z3, z3, z3, z3, z3, pp, mu_xx, mu_xg, wup, aup, gup)


DSA_TK = 512
DSA_HG = 2
LOG2E = 1.4426950408889634


def _dsa_kernel(q_ref, qi_ref, kwq_ref, ckv_ref, kwf_ref, kvn_ref, ikn_ref, wuk_ref, wuvt_ref,
                o_ref, ckv_s, ckvt_s, ke_s, ko_s, key_s, cut_s, qlat_s, m_s, l_s, acc_s, *, seq):
    j = pl.program_id(1)
    tk = DSA_TK
    qb = Q_BLOCK
    lane = lax.broadcasted_iota(jnp.int32, (1, LANES), 1)

    @pl.when(j == 0)
    def _():
        def prep(t, carry):
            sl = pl.ds(pl.multiple_of(t * tk, tk), tk)
            ckv = _rms(ckv_ref[sl, :], kvn_ref[...])
            ckv_s[sl, :] = ckv.astype(BF16)
            ckvt_s[t] = ckv.T.astype(BF16)
            kw = kwf_ref[sl, :]
            ksq = jnp.where(lane < IDX_HEAD_DIM, kw * kw, 0.0)
            ms = jnp.sum(ksq, axis=-1, keepdims=True) * (1.0 / IDX_HEAD_DIM)
            kn = kw * lax.rsqrt(ms + NORM_EPS) * ikn_ref[...]
            ke_s[sl, :] = kn.astype(BF16)
            ko_s[sl, :] = pltpu.roll(kn, IDX_HEAD_DIM, axis=1).astype(BF16)
            return carry
        lax.fori_loop(0, seq // tk, prep, 0)

    n_tiles = (j * qb + qb + tk - 1) // tk
    limit = j * qb + (lane // CHUNK + 1) * CHUNK
    krow = lax.broadcasted_iota(jnp.int32, (tk, 1), 0)

    w_t = kwq_ref[...].T * ((IDX_HEADS ** -0.5) * (IDX_HEAD_DIM ** -0.5))
    w_rows = [w_t[IDX_HEAD_DIM + h:IDX_HEAD_DIM + h + 1, :] for h in range(IDX_HEADS)]
    qi_cat = [jnp.concatenate([qi_ref[:, (2 * c) * LANES:(2 * c + 1) * LANES],
                               qi_ref[:, (2 * c + 1) * LANES:(2 * c + 2) * LANES]], axis=0).astype(BF16)
              for c in range(IDX_HEADS // 4)]

    def score_tile(t, carry):
        sl = pl.ds(pl.multiple_of(t * tk, tk), tk)
        ke = ke_s[sl, :]
        ko = ko_s[sl, :]
        sc = jnp.zeros((tk, qb), F32)
        for c in range(IDX_HEADS // 4):
            xe = jnp.maximum(_dot_nt(ke, qi_cat[c]), 0.0)
            xo = jnp.maximum(_dot_nt(ko, qi_cat[c]), 0.0)
            sc = sc + (xe[:, :qb] * w_rows[4 * c] + xe[:, qb:] * w_rows[4 * c + 2]
                       + xo[:, :qb] * w_rows[4 * c + 1] + xo[:, qb:] * w_rows[4 * c + 3])
        sc = jnp.where(t * tk + krow < limit, sc, -jnp.inf)
        bits = pltpu.bitcast(sc, jnp.int32)
        key_s[t] = bits ^ ((bits >> 31) & 0x7FFFFFFF)
        return carry

    lax.fori_loop(0, n_tiles, score_tile, 0)

    def count_where(pred):
        def body(t, part):
            hit = jnp.where(pred(key_s[t], t), 1.0, 0.0)
            return part + jnp.sum(hit.reshape(4, tk // 32, 8, qb), axis=1)
        part = lax.fori_loop(0, n_tiles, body, jnp.zeros((4, 8, qb), F32))
        return jnp.sum(jnp.sum(part, axis=0), axis=0, keepdims=True)

    def count_ge(cand):
        return count_where(lambda k, t: k >= cand)

    int_min = jnp.int32(-2 ** 31)
    topk = float(TOPK_MAX)
    theta = jnp.where(count_ge(jnp.zeros((1, qb), jnp.int32)) >= topk, 0, int_min)

    def bit_step(i, theta):
        cand = theta | (jnp.int32(1) << (30 - i))
        return jnp.where(count_ge(cand) >= topk, cand, theta)

    theta = lax.fori_loop(0, 31, bit_step, theta)

    seq_bits = (seq - 1).bit_length()
    cut_s[...] = jnp.full_like(cut_s, seq)
    n_ge = count_ge(theta)
    n_gt = n_ge - count_where(lambda k, t: k == theta)

    @pl.when(jnp.max(n_ge) > topk)
    def _():
        keep = topk - n_gt

        def idx_step(i, cut):
            cand = cut | (jnp.int32(1) << (seq_bits - 1 - i))
            below = count_where(lambda k, t: (k == theta) & (t * tk + krow < cand))
            return jnp.where(below <= keep - 1.0, cand, cut)

        cut_s[...] = lax.fori_loop(0, seq_bits, idx_step, jnp.zeros((1, qb), jnp.int32))

    cut = cut_s[...]

    for h in range(B_HEADS):
        qh = q_ref[:, h * B_HEAD_DIM:(h + 1) * B_HEAD_DIM].astype(BF16)
        qlat_s[h * qb:(h + 1) * qb, :] = (
            _dot(qh, wuk_ref[h]) * ((B_HEAD_DIM ** -0.5) * LOG2E)).astype(BF16)

    m_s[...] = jnp.full_like(m_s, NEG_BIG)
    l_s[...] = jnp.zeros_like(l_s)
    acc_s[...] = jnp.zeros_like(acc_s)
    gw = DSA_HG * qb
    n_groups = B_HEADS // DSA_HG

    def attend(t, carry):
        ckv = ckv_s[pl.ds(pl.multiple_of(t * tk, tk), tk), :]
        ckvt = ckvt_s[t]
        keys = key_s[t]
        kpos = t * tk + krow
        sel = ((keys > theta) | ((keys == theta) & (kpos <= cut))) & (kpos < limit)
        bias1 = jnp.where(sel, 0.0, NEG_BIG)
        bias = jnp.concatenate([bias1] * DSA_HG, axis=1)
        lgs = [_dot_nt(ckv, qlat_s[g * gw:(g + 1) * gw, :]) for g in range(n_groups)]
        for g in range(n_groups):
            lg = lgs[g] + bias
            cols = slice(g * gw, (g + 1) * gw)
            m_old = m_s[:, cols]
            m_new = jnp.maximum(m_old, jnp.max(lg, axis=0, keepdims=True))
            alpha = jnp.exp2(m_old - m_new)
            p = jnp.exp2(lg - m_new)
            l_s[:, cols] = alpha * l_s[:, cols] + jnp.sum(p, axis=0, keepdims=True)
            acc_s[:, cols] = alpha * acc_s[:, cols] + _dot(ckvt, p.astype(BF16))
            m_s[:, cols] = m_new
        return carry

    lax.fori_loop(0, n_tiles, attend, 0)

    for h in range(B_HEADS):
        cols = slice(h * qb, (h + 1) * qb)
        o_t = (acc_s[:, cols] * (1.0 / l_s[:, cols])).astype(BF16)
        o_ref[:, h * B_HEAD_DIM:(h + 1) * B_HEAD_DIM] = _dot(wuvt_ref[h], o_t).T.astype(o_ref.dtype)


def _dsa(z3, kv_norm, ikn_pad, w_uk, w_uv):
    bsz, seq, _ = z3.shape
    w_uv = jnp.swapaxes(w_uv, 1, 2)
    qb = Q_BLOCK
    return pl.pallas_call(
        functools.partial(_dsa_kernel, seq=seq),
        grid=(bsz, seq // qb),
        in_specs=[
            pl.BlockSpec((None, qb, B_WIDTH), lambda b, j: (b, j, COL_Q // B_WIDTH)),
            pl.BlockSpec((None, qb, B_WIDTH), lambda b, j: (b, j, COL_QI // B_WIDTH)),
            pl.BlockSpec((None, qb, LANES), lambda b, j: (b, j, COL_KW // LANES)),
            pl.BlockSpec((None, seq, KV_LATENT), lambda b, j: (b, 0, COL_CKV // KV_LATENT)),
            pl.BlockSpec((None, seq, LANES), lambda b, j: (b, 0, COL_KW // LANES)),
            pl.BlockSpec((1, KV_LATENT), lambda b, j: (0, 0)),
            pl.BlockSpec((1, LANES), lambda b, j: (0, 0)),
            pl.BlockSpec(w_uk.shape, lambda b, j: (0, 0, 0)),
            pl.BlockSpec(w_uv.shape, lambda b, j: (0, 0, 0)),
        ],
        out_specs=pl.BlockSpec((None, qb, B_WIDTH), lambda b, j: (b, j, 0)),
        out_shape=jax.ShapeDtypeStruct((bsz, seq, B_WIDTH), BF16),
        scratch_shapes=[
            pltpu.VMEM((seq, KV_LATENT), BF16),
            pltpu.VMEM((seq // DSA_TK, KV_LATENT, DSA_TK), BF16),
            pltpu.VMEM((seq, LANES), BF16),
            pltpu.VMEM((seq, LANES), BF16),
            pltpu.VMEM((seq // DSA_TK, DSA_TK, qb), jnp.int32),
            pltpu.VMEM((1, qb), jnp.int32),
            pltpu.VMEM((B_HEADS * qb, KV_LATENT), BF16),
            pltpu.VMEM((1, B_HEADS * qb), F32),
            pltpu.VMEM((1, B_HEADS * qb), F32),
            pltpu.VMEM((KV_LATENT, B_HEADS * qb), F32),
        ],
        compiler_params=pltpu.CompilerParams(
            dimension_semantics=("parallel", "arbitrary"), vmem_limit_bytes=VMEM_LIMIT),
        name="dsa",
    )(z3, z3, z3, z3, z3, kv_norm, ikn_pad, w_uk, w_uv)


def _pad_cols(w, width):
    return jnp.pad(w, ((0, 0), (0, width - w.shape[1])))


def _layer0_params(ab_w_in, mu, w0, w_up, a0, a_up, g_up, k_k, k_a, r_k, gn_g, gn_b, idx_k_norm):
    a_in = 3 * A_WIDTH + LORA_W + LORA_A + LORA_G
    o = 3 * A_WIDTH
    ab_w_in = ab_w_in.astype(BF16)
    w_r, w_k, w_v = ab_w_in[:, :A_WIDTH], ab_w_in[:, A_WIDTH:2 * A_WIDTH], ab_w_in[:, 2 * A_WIDTH:o]
    w_xwxa = ab_w_in[:, o:o + LORA_W + LORA_A]
    w_xg = ab_w_in[:, o + LORA_W + LORA_A:a_in]
    wb = ab_w_in[:, a_in:]
    o1 = B_WIDTH
    o2 = o1 + KV_LATENT
    o3 = o2 + IDX_HEADS * IDX_HEAD_DIM
    w_q, w_ckv, w_qi, w_kw = wb[:, :o1], wb[:, o1:o2], wb[:, o2:o3], wb[:, o3:]
    w_in = jnp.concatenate(
        [w_r, w_k, w_v, w_q, w_qi, w_ckv, _pad_cols(w_xg, 256), w_xwxa, _pad_cols(w_kw, LANES)], axis=1)
    w_in = _pad_cols(w_in, AB_IN_PAD)

    rows = [mu[:A_WIDTH], mu[A_WIDTH:2 * A_WIDTH], mu[2 * A_WIDTH:o], w0, a0, k_k, k_a,
            r_k.reshape(-1), gn_g, gn_b]
    pp = jnp.stack(rows + [jnp.zeros_like(w0)] * (RW_PP_ROWS - len(rows)), axis=0)
    mu_xx = mu[o:o + LORA_W + LORA_A][None, :]
    mu_xg = jnp.pad(mu[o + LORA_W + LORA_A:a_in], (0, 256 - LORA_G))[None, :]
    wup = jnp.pad(w_up, ((0, LANES - LORA_W), (0, 0))).astype(BF16)
    aup = jnp.pad(a_up, ((LORA_W, 0), (0, 0))).astype(BF16)
    gup = jnp.pad(g_up, ((0, 256 - LORA_G), (0, 0))).astype(BF16)
    ikn = jnp.pad(idx_k_norm, (0, LANES - IDX_HEAD_DIM))[None, :]
    return w_in, pp, mu_xx, mu_xg, wup, aup, gup, ikn


def kernel(x, norm_g, ab_w_in, ab_shift_mu, rw_w0, rw_w_up, rw_a0, rw_a_up, rw_g_up, rw_k_k, rw_k_a, rw_r_k, rw_gn_g, rw_gn_b, mla_kv_norm, mla_w_uk, mla_w_uv, idx_k_norm, ab_w_out, sg_w_in, sg_b_in, sg_ln_g, sg_ln_b, sg_w_s, sg_b_s, sg_w_out, ffn_w_up, ffn_conv_w, ffn_conv_b, ffn_w_down):
    bsz, seq, d = x.shape
    n = bsz * seq
    depth = norm_g.shape[0]
    h = x.reshape(n, d)
    for layer in range(depth):
        g_pre_mix, g_post_mix, g_pre_ffn, g_post_ffn = [norm_g[layer, i][None, :] for i in range(4)]
        if layer % 2 == 0:
            e = layer // 2
            w_in, pp, mu_xx, mu_xg, wup, aup, gup, ikn = _layer0_params(
                ab_w_in[e], ab_shift_mu[e], rw_w0[e], rw_w_up[e], rw_a0[e], rw_a_up[e], rw_g_up[e],
                rw_k_k[e], rw_k_a[e], rw_r_k[e], rw_gn_g[e], rw_gn_b[e], idx_k_norm[e])
            z = _norm_matmul(h, g_pre_mix, w_in, jnp.zeros((1, AB_IN_PAD), F32), gelu=False)
            z3 = z.reshape(bsz, seq, AB_IN_PAD)
            y_a = _rwkv(z3, pp, mu_xx, mu_xg, wup, aup, gup)
            y_b = _dsa(z3, mla_kv_norm[e][None, :], ikn, mla_w_uk[e].astype(BF16),
                       mla_w_uv[e].astype(BF16))
            h = _outproj(y_a.reshape(n, A_WIDTH), y_b.reshape(n, B_WIDTH),
                         ab_w_out[e].astype(BF16), h, g_post_mix)
        else:
            o = layer // 2
            z = _norm_matmul(h, g_pre_mix, sg_w_in[o].astype(BF16), sg_b_in[o][None, :], gelu=True)
            h = _sgu(z, sg_ln_g[o][None, :], sg_ln_b[o][None, :], sg_w_s[o], sg_b_s[o].T,
                     sg_w_out[o].astype(BF16), h, g_post_mix)
        h = _ffn(h, g_pre_ffn, g_post_ffn, ffn_w_up[layer].astype(BF16), ffn_conv_w[layer],
                 ffn_conv_b[layer][None, :], ffn_w_down[layer].astype(BF16), seq=seq)
    return h.reshape(bsz, seq, d)
```

```python
import functools

import jax
import jax.numpy as jnp
from jax import lax
from jax.experimental import pallas as pl
from jax.experimental.pallas import tpu as pltpu

F32 = jnp.float32
BF16 = jnp.bfloat16

NORM_EPS = 1e-6
CHUNK = 64
A_HEADS = 16
A_HEAD_DIM = 64
A_WIDTH = A_HEADS * A_HEAD_DIM
LORA_W = 64
LORA_A = 64
LORA_G = 160
A_GN_EPS = 64e-5
B_HEADS = 8
B_HEAD_DIM = 128
B_WIDTH = B_HEADS * B_HEAD_DIM
KV_LATENT = 256
IDX_HEADS = 16
IDX_HEAD_DIM = 64
TOPK_MAX = 256
SG_CHUNK = 128
SG_GROUPS = 8
CONV_W = 3

LANES = 128
BF16_SUBLANES = 16
VMEM_LIMIT = 56 * 1024 * 1024

COL_R = 0
COL_K = A_WIDTH
COL_V = 2 * A_WIDTH
COL_Q = 3 * A_WIDTH
COL_QI = COL_Q + B_WIDTH
COL_CKV = COL_QI + IDX_HEADS * IDX_HEAD_DIM
COL_XG = COL_CKV + KV_LATENT
COL_XWXA = COL_XG + 256
COL_KW = COL_XWXA + LANES
AB_IN_PAD = 6144

RW_CHUNK = 64
NEG_BIG = -1e30


def _rms(x, g):
    ms = jnp.mean(x * x, axis=-1, keepdims=True)
    return x * lax.rsqrt(ms + NORM_EPS) * g


def _dot(a, b):
    return jnp.dot(a, b, preferred_element_type=F32)


def _dot_nt(a, b):
    return lax.dot_general(a, b, (((1,), (1,)), ((), ())), preferred_element_type=F32)


def _dot_tn(a, b):
    return lax.dot_general(a, b, (((0,), (0,)), ((), ())), preferred_element_type=F32)


def _split2(x):
    hi = x.astype(BF16)
    lo = (x - hi.astype(F32)).astype(BF16)
    return hi, lo


def _split3(x):
    hi = x.astype(BF16)
    r1 = x - hi.astype(F32)
    mid = r1.astype(BF16)
    lo = (r1 - mid.astype(F32)).astype(BF16)
    return hi, mid, lo


def _norm_mm_kernel(x_ref, g_ref, w_ref, b_ref, o_ref, xn_ref, *, gelu):
    @pl.when(pl.program_id(1) == 0)
    def _():
        xn_ref[...] = _rms(x_ref[...], g_ref[...]).astype(BF16)

    y = _dot(xn_ref[...], w_ref[...])
    if gelu:
        y = y + b_ref[...]
        y = 0.5 * y * (1.0 + lax.erf(y * 0.7071067811865476))
    o_ref[...] = y.astype(o_ref.dtype)


def _norm_matmul(x, g, w, b, *, gelu, tm=1024, tn=1024, out_dtype=F32):
    m, k = x.shape
    n = w.shape[1]
    return pl.pallas_call(
        functools.partial(_norm_mm_kernel, gelu=gelu),
        grid=(m // tm, n // tn),
        in_specs=[
            pl.BlockSpec((tm, k), lambda i, j: (i, 0)),
            pl.BlockSpec((1, k), lambda i, j: (0, 0)),
            pl.BlockSpec((k, tn), lambda i, j: (0, j)),
            pl.BlockSpec((1, tn), lambda i, j: (0, j)),
        ],
        out_specs=pl.BlockSpec((tm, tn), lambda i, j: (i, j)),
        out_shape=jax.ShapeDtypeStruct((m, n), out_dtype),
        scratch_shapes=[pltpu.VMEM((tm, k), BF16)],
        compiler_params=pltpu.CompilerParams(
            dimension_semantics=("parallel", "arbitrary"), vmem_limit_bytes=VMEM_LIMIT),
        name="norm_matmul",
    )(x, g, w, b)


FFN_HALO = BF16_SUBLANES


def _ffn_kernel(x_ref, xp_ref, gpre_ref, wg_ref, wu_ref, cwg_ref, cwu_ref, cbg_ref, cbu_ref,
                wd_ref, gpost_ref, o_ref, xn_ref, hg_ref, hu_ref, acc_ref, *, tm, tiles_per_seq):
    i = pl.program_id(0)
    j = pl.program_id(1)

    @pl.when(j == 0)
    def _():
        xn_ref[FFN_HALO:, :] = _rms(x_ref[...], gpre_ref[...]).astype(BF16)
        keep = jnp.where(i % tiles_per_seq != 0, 1.0, 0.0)
        xn_ref[:FFN_HALO, :] = (_rms(xp_ref[...], gpre_ref[...]) * keep).astype(BF16)
        acc_ref[...] = jnp.zeros_like(acc_ref)

    xn = xn_ref[...]
    hg_ref[...] = _dot(xn, wg_ref[...])
    hu_ref[...] = _dot(xn, wu_ref[...])

    def conv(h_ref, cw_ref, cb_ref):
        cw = cw_ref[...]
        return (h_ref[pl.ds(FFN_HALO, tm), :] * cw[2:3, :]
                + h_ref[pl.ds(FFN_HALO - 1, tm), :] * cw[1:2, :]
                + h_ref[pl.ds(FFN_HALO - 2, tm), :] * cw[0:1, :]
                + cb_ref[...])

    gate = conv(hg_ref, cwg_ref, cbg_ref)
    up = conv(hu_ref, cwu_ref, cbu_ref)
    act = gate * jax.nn.sigmoid(gate) * up
    acc_ref[...] += _dot(act.astype(BF16), wd_ref[...])

    @pl.when(j == pl.num_programs(1) - 1)
    def _():
        o_ref[...] = x_ref[...] + _rms(acc_ref[...], gpost_ref[...])


def _ffn(h, g_pre, g_post, w_up, conv_w, conv_b, w_down, layer, *, seq, tm=512, tf=512):
    m, d = h.shape
    d_ff = w_down.shape[1]
    nf = d_ff // tf
    halo_blocks = tm // FFN_HALO
    return pl.pallas_call(
        functools.partial(_ffn_kernel, tm=tm, tiles_per_seq=seq // tm),
        grid=(m // tm, nf),
        in_specs=[
            pl.BlockSpec((tm, d), lambda i, j: (i, 0)),
            pl.BlockSpec((FFN_HALO, d), lambda i, j: (jnp.maximum(i * halo_blocks - 1, 0), 0)),
            pl.BlockSpec((1, d), lambda i, j: (0, 0)),
            pl.BlockSpec((None, d, tf), lambda i, j: (layer, 0, j)),
            pl.BlockSpec((None, d, tf), lambda i, j: (layer, 0, j + nf)),
            pl.BlockSpec((None, CONV_W, tf), lambda i, j: (layer, 0, j)),
            pl.BlockSpec((None, CONV_W, tf), lambda i, j: (layer, 0, j + nf)),
            pl.BlockSpec((None, 1, tf), lambda i, j: (layer, 0, j)),
            pl.BlockSpec((None, 1, tf), lambda i, j: (layer, 0, j + nf)),
            pl.BlockSpec((None, tf, d), lambda i, j: (layer, j, 0)),
            pl.BlockSpec((1, d), lambda i, j: (0, 0)),
        ],
        out_specs=pl.BlockSpec((tm, d), lambda i, j: (i, 0)),
        out_shape=jax.ShapeDtypeStruct((m, d), F32),
        scratch_shapes=[
            pltpu.VMEM((tm + FFN_HALO, d), BF16),
            pltpu.VMEM((tm + FFN_HALO, tf), F32),
            pltpu.VMEM((tm + FFN_HALO, tf), F32),
            pltpu.VMEM((tm, d), F32),
        ],
        compiler_params=pltpu.CompilerParams(
            dimension_semantics=("parallel", "arbitrary"), vmem_limit_bytes=VMEM_LIMIT),
        name="conv_ffn",
    )(h, h, g_pre, w_up, w_up, conv_w, conv_w, conv_b, conv_b, w_down, g_post)


def _outproj_kernel(ya_ref, yb_ref, w_ref, h_ref, g_ref, o_ref):
    ka = ya_ref.shape[1]
    mix = _dot(ya_ref[...], w_ref[:ka, :]) + _dot(yb_ref[...], w_ref[ka:, :])
    o_ref[...] = h_ref[...] + _rms(mix, g_ref[...])


def _outproj(ya, yb, w, h, g, *, tm=512):
    m, d = h.shape
    return pl.pallas_call(
        _outproj_kernel,
        grid=(m // tm,),
        in_specs=[
            pl.BlockSpec((tm, ya.shape[1]), lambda i: (i, 0)),
            pl.BlockSpec((tm, yb.shape[1]), lambda i: (i, 0)),
            pl.BlockSpec(w.shape, lambda i: (0, 0)),
            pl.BlockSpec((tm, d), lambda i: (i, 0)),
            pl.BlockSpec((1, d), lambda i: (0, 0)),
        ],
        out_specs=pl.BlockSpec((tm, d), lambda i: (i, 0)),
        out_shape=jax.ShapeDtypeStruct((m, d), F32),
        compiler_params=pltpu.CompilerParams(
            dimension_semantics=("parallel",), vmem_limit_bytes=VMEM_LIMIT),
        name="outproj",
    )(ya, yb, w, h, g)


def _sgu_kernel(u_ref, v_ref, lng_ref, lnb_ref, ws_ref, bs_ref, wo_ref, h_ref, g_ref, o_ref,
                vln_ref, gated_ref, *, tm):
    v = v_ref[...]
    mu = jnp.mean(v, axis=-1, keepdims=True)
    dv = v - mu
    var = jnp.mean(dv * dv, axis=-1, keepdims=True)
    vln_ref[...] = (dv * lax.rsqrt(var + 1e-5) * lng_ref[...] + lnb_ref[...]).astype(BF16)

    gw = v.shape[1] // SG_GROUPS
    ii = lax.broadcasted_iota(jnp.int32, (SG_CHUNK, SG_CHUNK), 0)
    jj = lax.broadcasted_iota(jnp.int32, (SG_CHUNK, SG_CHUNK), 1)
    visible = (jj // CHUNK) <= (ii // CHUNK)
    for g in range(SG_GROUPS):
        wm = jnp.where(visible, ws_ref[g], 0.0).astype(BF16)
        bcol = bs_ref[:, g:g + 1]
        cols = slice(g * gw, (g + 1) * gw)
        for c in range(tm // SG_CHUNK):
            rows = slice(c * SG_CHUNK, (c + 1) * SG_CHUNK)
            s = _dot(wm, vln_ref[rows, cols]) + bcol
            gated_ref[rows, cols] = (u_ref[rows, cols] * s).astype(BF16)

    mix = _dot(gated_ref[...], wo_ref[...])
    o_ref[...] = h_ref[...] + _rms(mix, g_ref[...])


def _sgu(z, ln_g, ln_b, w_s, b_s_t, w_out, h, g, *, tm=256):
    m, d = h.shape
    return pl.pallas_call(
        functools.partial(_sgu_kernel, tm=tm),
        grid=(m // tm,),
        in_specs=[
            pl.BlockSpec((tm, d), lambda i: (i, 0)),
            pl.BlockSpec((tm, d), lambda i: (i, 1)),
            pl.BlockSpec((1, d), lambda i: (0, 0)),
            pl.BlockSpec((1, d), lambda i: (0, 0)),
            pl.BlockSpec(w_s.shape, lambda i: (0, 0, 0)),
            pl.BlockSpec(b_s_t.shape, lambda i: (0, 0)),
            pl.BlockSpec(w_out.shape, lambda i: (0, 0)),
            pl.BlockSpec((tm, d), lambda i: (i, 0)),
            pl.BlockSpec((1, d), lambda i: (0, 0)),
        ],
        out_specs=pl.BlockSpec((tm, d), lambda i: (i, 0)),
        out_shape=jax.ShapeDtypeStruct((m, d), F32),
        scratch_shapes=[pltpu.VMEM((tm, d), BF16), pltpu.VMEM((tm, d), BF16)],
        compiler_params=pltpu.CompilerParams(
            dimension_semantics=("parallel",), vmem_limit_bytes=VMEM_LIMIT),
        name="sgu",
    )(z, z, ln_g, ln_b, w_s, b_s_t, w_out, h, g)


RW_PP_ROWS = 16
(PP_MU_R, PP_MU_K, PP_MU_V, PP_W0, PP_A0, PP_KK, PP_KA, PP_RK, PP_GNG, PP_GNB) = range(10)
RW_CARRY = 8
RW_GROUP = 8


def _rwkv_kernel(r_ref, k_ref, v_ref, xx_ref, xg_ref, pp_ref, mux_ref, mug_ref, wup_ref, aup_ref,
                 gup_ref, y_ref,
                 rb_ref, kb_ref, vb_ref, xb_ref, gb_ref,
                 r_s, k_s, v_s, lw_s, kk_s, be_s, g_s, bonus_s, y_s, ra_s, z_s, phi_s, psi_s, state_ref,
                 *, tt):
    L = RW_CHUNK
    tb = pl.program_id(2)
    pp = pp_ref[...]
    row = lambda i: pp[i:i + 1, :]

    lane = lax.broadcasted_iota(jnp.int32, (1, LANES), 1)
    head_masks = (lane < A_HEAD_DIM, lane >= A_HEAD_DIM)
    li = lax.broadcasted_iota(jnp.int32, (LANES, LANES), 0)
    lj = lax.broadcasted_iota(jnp.int32, (LANES, LANES), 1)
    same_head = (li // A_HEAD_DIM) == (lj // A_HEAD_DIM)

    def seg_sum(x):
        s0 = jnp.sum(jnp.where(head_masks[0], x, 0.0), axis=-1, keepdims=True)
        s1 = jnp.sum(jnp.where(head_masks[1], x, 0.0), axis=-1, keepdims=True)
        return jnp.where(head_masks[0], s0, s1)

    @pl.when(tb == 0)
    def _():
        for b in (rb_ref, kb_ref, vb_ref, xb_ref, gb_ref):
            b[0:RW_CARRY, :] = jnp.zeros((RW_CARRY, b.shape[1]), F32)
        state_ref[...] = jnp.zeros_like(state_ref)

    def shift_mix(x_ref, buf_ref, mu):
        x = x_ref[...]
        buf_ref[RW_CARRY:, :] = x
        prev = buf_ref[pl.ds(RW_CARRY - 1, tt), :]
        return x + (prev - x) * mu

    r = shift_mix(r_ref, rb_ref, row(PP_MU_R))
    k = shift_mix(k_ref, kb_ref, row(PP_MU_K))
    v = shift_mix(v_ref, vb_ref, row(PP_MU_V))
    xx = shift_mix(xx_ref, xb_ref, mux_ref[...])
    xg = shift_mix(xg_ref, gb_ref, mug_ref[...])
    for src, buf in ((r_ref, rb_ref), (k_ref, kb_ref), (v_ref, vb_ref), (xx_ref, xb_ref),
                     (xg_ref, gb_ref)):
        buf[0:RW_CARRY, :] = src[tt - RW_CARRY:tt, :]

    w_pre = row(PP_W0) + _dot(jnp.tanh(xx).astype(BF16), wup_ref[...])
    log_w = -(jnp.maximum(-w_pre, 0.0) + jnp.log1p(jnp.exp(-jnp.abs(w_pre)))) - 0.5
    lw_s[...] = -jnp.exp(log_w)
    a = jax.nn.sigmoid(row(PP_A0) + _dot(xx.astype(BF16), aup_ref[...]))
    g_s[...] = _dot(jax.nn.sigmoid(xg).astype(BF16), gup_ref[...])
    kk = k * row(PP_KK)
    kk = kk * lax.rsqrt(seg_sum(kk * kk) + 1e-12)
    k = k * (1.0 + (a - 1.0) * row(PP_KA))
    bonus_s[...] = seg_sum(r * k * row(PP_RK)) * v
    r_s[...] = r
    k_s[...] = k
    v_s[...] = v
    kk_s[...] = kk
    be_s[...] = kk * a

    tl = li % L
    cl = lj % L
    strict_bd = same_head & (cl < tl)
    incl_bd = same_head & (cl <= tl)
    tri_bd = jnp.where(incl_bd, 1.0, 0.0).astype(BF16)
    eye = jnp.where(li == lj, 1.0, 0.0).astype(F32)

    def stack(x):
        return jnp.concatenate([jnp.where(head_masks[0], x, 0.0), jnp.where(head_masks[1], x, 0.0)],
                               axis=0)

    state = [state_ref[...]]
    pending = []

    def tick():
        if pending:
            pending.pop(0)()

    def make_step(c):
        def step():
            s_hi, s_lo = _split2(state[0])
            y_st = _dot_nt(ra_s[c].astype(BF16), s_hi) + z_s[c]
            y_s[c * L:(c + 1) * L, :] = y_st[:L, :] + y_st[L:, :]
            p_hi, p_lo = _split2(phi_s[c])
            state[0] = _dot(s_hi, p_hi) + _dot(s_lo, p_hi) + _dot(s_hi, p_lo) + psi_s[c]
        return step

    def chunk_group(gi):
        G = range(RW_GROUP)
        cs, lw = [], []
        for half in range(RW_GROUP // 2):
            c0 = gi * RW_GROUP + 2 * half
            lw2 = lw_s[c0 * L:(c0 + 2) * L, :]
            h3 = _split3(lw2)
            cs2 = _dot(tri_bd, h3[0]) + _dot(tri_bd, h3[1]) + _dot(tri_bd, h3[2])
            cs += [cs2[:L, :], cs2[L:, :]]
            lw += [lw2[:L, :], lw2[L:, :]]
        tick()
        cidx = [gi * RW_GROUP + i for i in G]
        sls = [slice(c * L, (c + 1) * L) for c in cidx]
        e_pos = [jnp.exp(cs[i]) for i in G]
        e_prev = [jnp.exp(cs[i] - lw[i]) for i in G]
        e_neg = [jnp.exp(-cs[i]) for i in G]
        lam_l = [e_pos[i][L - 1:L, :] for i in G]
        rb_f = [stack(r_s[sls[i], :] * e_pos[i]) for i in G]
        ab = [stack(-kk_s[sls[i], :] * e_prev[i]).astype(BF16) for i in G]
        rb = [rb_f[i].astype(BF16) for i in G]
        bt = [stack(be_s[sls[i], :] * e_neg[i]).astype(BF16) for i in G]
        kt = [stack(k_s[sls[i], :] * e_neg[i]).astype(BF16) for i in G]
        v16 = [stack(v_s[sls[i], :]).astype(BF16) for i in G]
        tick()
        g = [_dot_nt(jnp.concatenate([ab[i], rb[i]], axis=0), jnp.concatenate([bt[i], kt[i]], axis=0))
             for i in G]
        a_ab = [jnp.where(strict_bd, g[i][:LANES, :LANES], 0.0) for i in G]
        a_ak = [jnp.where(strict_bd, g[i][:LANES, LANES:], 0.0).astype(BF16) for i in G]
        a_rb = [jnp.where(incl_bd, g[i][LANES:, :LANES], 0.0).astype(BF16) for i in G]
        a_rk = [jnp.where(incl_bd, g[i][LANES:, LANES:], 0.0).astype(BF16) for i in G]
        tick()
        t = [eye + a_ab[i] for i in G]
        p = a_ab
        for _ in range(5):
            p16 = [p[i].astype(BF16) for i in G]
            p = [_dot(p16[i], p16[i]) for i in G]
            tick()
            t = [t[i] + _dot(t[i].astype(BF16), p[i].astype(BF16)) for i in G]
            tick()
        t16 = [t[i].astype(BF16) for i in G]
        av = [_dot(a_ak[i], v16[i]).astype(BF16) for i in G]
        ta = [_dot(t16[i], ab[i]).astype(BF16) for i in G]
        tick()
        w16 = [_dot(t16[i], av[i]).astype(BF16) for i in G]
        for i in G:
            ra_s[cidx[i]] = rb_f[i] + _dot(a_rb[i], ta[i])
        tick()
        for i in G:
            z_s[cidx[i]] = _dot(a_rb[i], w16[i]) + _dot(a_rk[i], v16[i])
        tick()
        for i in G:
            phi_s[cidx[i]] = (eye + _dot_tn(ta[i], bt[i])) * lam_l[i]
        tick()
        for i in G:
            psi_s[cidx[i]] = (_dot_tn(w16[i], bt[i]) + _dot_tn(v16[i], kt[i])) * lam_l[i]
        while pending:
            tick()
        pending.extend(make_step(c) for c in cidx)

    for gi in range(tt // (L * RW_GROUP)):
        chunk_group(gi)
    while pending:
        tick()
    state_ref[...] = state[0]

    y = y_s[...]
    inv_n = 1.0 / A_HEAD_DIM
    mean = seg_sum(y) * inv_n
    d = y - mean
    var = seg_sum(d * d) * inv_n
    yn = d * lax.rsqrt(var + A_GN_EPS) * row(PP_GNG) + row(PP_GNB)
    y_ref[...] = ((yn + bonus_s[...]) * g_s[...]).astype(y_ref.dtype)


def _rwkv(z3, pp, mu_xx, mu_xg, wup, aup, gup, *, tt=1024):
    bsz, seq, _ = z3.shape
    pairs = A_WIDTH // LANES
    cb = lambda col, w: col // w
    col_spec = lambda col0: pl.BlockSpec((None, tt, LANES), lambda b, p, t: (b, t, cb(col0, LANES) + p))
    buf = lambda w: pltpu.VMEM((tt + RW_CARRY, w), F32)
    big = pltpu.VMEM((tt, LANES), F32)
    return pl.pallas_call(
        functools.partial(_rwkv_kernel, tt=tt),
        grid=(bsz, pairs, seq // tt),
        in_specs=[
            col_spec(COL_R), col_spec(COL_K), col_spec(COL_V),
            pl.BlockSpec((None, tt, LANES), lambda b, p, t: (b, t, cb(COL_XWXA, LANES))),
            pl.BlockSpec((None, tt, 256), lambda b, p, t: (b, t, cb(COL_XG, 256))),
            pl.BlockSpec((RW_PP_ROWS, LANES), lambda b, p, t: (0, p)),
            pl.BlockSpec((1, LANES), lambda b, p, t: (0, 0)),
            pl.BlockSpec((1, 256), lambda b, p, t: (0, 0)),
            pl.BlockSpec((LANES, LANES), lambda b, p, t: (0, p)),
            pl.BlockSpec((LANES, LANES), lambda b, p, t: (0, p)),
            pl.BlockSpec((256, LANES), lambda b, p, t: (0, p)),
        ],
        out_specs=pl.BlockSpec((None, tt, LANES), lambda b, p, t: (b, t, p)),
        out_shape=jax.ShapeDtypeStruct((bsz, seq, A_WIDTH), BF16),
        scratch_shapes=[buf(LANES), buf(LANES), buf(LANES), buf(LANES), buf(256)]
        + [big] * 9 + [pltpu.VMEM((tt // RW_CHUNK, LANES, LANES), F32)] * 4
        + [pltpu.VMEM((LANES, LANES), F32)],
        compiler_params=pltpu.CompilerParams(
            dimension_semantics=("parallel", "parallel", "arbitrary"), vmem_limit_bytes=VMEM_LIMIT),
        name="rwkv7",
    )(z3, z3, z3, z3, z3, pp, mu_xx, mu_xg, wup, aup, gup)


DSA_TK = 512
DSA_QB = 256
DSA_HG = 1
LOG2E = 1.4426950408889634


def _dsa_kernel(q_ref, qi_ref, kwq_ref, ckv_ref, kwf_ref, kvn_ref, ikn_ref, wuk_ref, wuvt_ref,
                o_ref, ckv_s, ckvt_s, ke_s, ko_s, key_s, cut_s, qlat_s, m_s, l_s, acc_s, *, seq):
    j = pl.program_id(1)
    tk = DSA_TK
    qb = DSA_QB
    lane = lax.broadcasted_iota(jnp.int32, (1, LANES), 1)
    qlane = lax.broadcasted_iota(jnp.int32, (1, qb), 1)

    @pl.when(j == 0)
    def _():
        def prep(t, carry):
            sl = pl.ds(pl.multiple_of(t * tk, tk), tk)
            ckv = _rms(ckv_ref[sl, :], kvn_ref[...])
            ckv_s[sl, :] = ckv.astype(BF16)
            ckvt_s[t] = ckv.T.astype(BF16)
            kw = kwf_ref[sl, :]
            ksq = jnp.where(lane < IDX_HEAD_DIM, kw * kw, 0.0)
            ms = jnp.sum(ksq, axis=-1, keepdims=True) * (1.0 / IDX_HEAD_DIM)
            kn = kw * lax.rsqrt(ms + NORM_EPS) * ikn_ref[...]
            ke_s[sl, :] = kn.astype(BF16)
            ko_s[sl, :] = pltpu.roll(kn, IDX_HEAD_DIM, axis=1).astype(BF16)
            return carry
        lax.fori_loop(0, seq // tk, prep, 0)

    n_tiles = (j * qb + qb + tk - 1) // tk
    limit = j * qb + (qlane // CHUNK + 1) * CHUNK
    krow = lax.broadcasted_iota(jnp.int32, (tk, 1), 0)

    w_t = kwq_ref[...].T * ((IDX_HEADS ** -0.5) * (IDX_HEAD_DIM ** -0.5))
    w_rows = [w_t[IDX_HEAD_DIM + h:IDX_HEAD_DIM + h + 1, :] for h in range(IDX_HEADS)]
    qi16 = [qi_ref[:, p * LANES:(p + 1) * LANES].astype(BF16) for p in range(IDX_HEADS // 2)]

    def score_tile(t, carry):
        sl = pl.ds(pl.multiple_of(t * tk, tk), tk)
        ke = ke_s[sl, :]
        ko = ko_s[sl, :]
        sc = jnp.zeros((tk, qb), F32)
        for p in range(IDX_HEADS // 2):
            xe = jnp.maximum(_dot_nt(ke, qi16[p]), 0.0)
            xo = jnp.maximum(_dot_nt(ko, qi16[p]), 0.0)
            sc = sc + (xe * w_rows[2 * p] + xo * w_rows[2 * p + 1])
        sc = jnp.where(t * tk + krow < limit, sc, -jnp.inf)
        bits = pltpu.bitcast(sc, jnp.int32)
        key_s[t] = bits ^ ((bits >> 31) & 0x7FFFFFFF)
        return carry

    lax.fori_loop(0, n_tiles, score_tile, 0)

    def count_where(pred):
        def body(t, part):
            hit = jnp.where(pred(key_s[t], t), 1.0, 0.0)
            return part + jnp.sum(hit.reshape(4, tk // 32, 8, qb), axis=1)
        part = lax.fori_loop(0, n_tiles, body, jnp.zeros((4, 8, qb), F32))
        return jnp.sum(jnp.sum(part, axis=0), axis=0, keepdims=True)

    def count_ge(cand):
        return count_where(lambda k, t: k >= cand)

    int_min = jnp.int32(-2 ** 31)
    topk = float(TOPK_MAX)
    theta = jnp.where(count_ge(jnp.zeros((1, qb), jnp.int32)) >= topk, 0, int_min)

    def bit_step(i, theta):
        cand = theta | (jnp.int32(1) << (30 - i))
        return jnp.where(count_ge(cand) >= topk, cand, theta)

    theta = lax.fori_loop(0, 31, bit_step, theta)

    seq_bits = (seq - 1).bit_length()
    cut_s[...] = jnp.full_like(cut_s, seq)
    n_ge = count_ge(theta)

    @pl.when(jnp.max(n_ge) > topk)
    def _():
        n_gt = n_ge - count_where(lambda k, t: k == theta)
        keep = topk - n_gt

        def idx_step(i, cut):
            cand = cut | (jnp.int32(1) << (seq_bits - 1 - i))
            below = count_where(lambda k, t: (k == theta) & (t * tk + krow < cand))
            return jnp.where(below <= keep - 1.0, cand, cut)

        cut_s[...] = lax.fori_loop(0, seq_bits, idx_step, jnp.zeros((1, qb), jnp.int32))

    cut = cut_s[...]

    for h in range(B_HEADS):
        qh = q_ref[:, h * B_HEAD_DIM:(h + 1) * B_HEAD_DIM].astype(BF16)
        qlat_s[h * qb:(h + 1) * qb, :] = (
            _dot(qh, wuk_ref[h]) * ((B_HEAD_DIM ** -0.5) * LOG2E)).astype(BF16)

    m_s[...] = jnp.full_like(m_s, NEG_BIG)
    l_s[...] = jnp.zeros_like(l_s)
    acc_s[...] = jnp.zeros_like(acc_s)
    gw = DSA_HG * qb
    n_groups = B_HEADS // DSA_HG

    def attend(t, carry):
        ckv = ckv_s[pl.ds(pl.multiple_of(t * tk, tk), tk), :]
        ckvt = ckvt_s[t]
        keys = key_s[t]
        kpos = t * tk + krow
        sel = ((keys > theta) | ((keys == theta) & (kpos <= cut))) & (kpos < limit)
        bias1 = jnp.where(sel, 0.0, NEG_BIG)
        bias = bias1 if DSA_HG == 1 else jnp.concatenate([bias1] * DSA_HG, axis=1)
        lgs = [_dot_nt(ckv, qlat_s[g * gw:(g + 1) * gw, :]) for g in range(n_groups)]
        for g in range(n_groups):
            lg = lgs[g] + bias
            cols = slice(g * gw, (g + 1) * gw)
            m_old = m_s[:, cols]
            m_new = jnp.maximum(m_old, jnp.max(lg, axis=0, keepdims=True))
            alpha = jnp.exp2(m_old - m_new)
            p = jnp.exp2(lg - m_new)
            l_s[:, cols] = alpha * l_s[:, cols] + jnp.sum(p, axis=0, keepdims=True)
            acc_s[:, cols] = alpha * acc_s[:, cols] + _dot(ckvt, p.astype(BF16))
            m_s[:, cols] = m_new
        return carry

    lax.fori_loop(0, n_tiles, attend, 0)

    for h in range(B_HEADS):
        cols = slice(h * qb, (h + 1) * qb)
        o_t = (acc_s[:, cols] * (1.0 / l_s[:, cols])).astype(BF16)
        o_ref[:, h * B_HEAD_DIM:(h + 1) * B_HEAD_DIM] = _dot(wuvt_ref[h], o_t).T.astype(o_ref.dtype)


def _dsa(z3, kv_norm, ikn_pad, w_uk, w_uv):
    bsz, seq, _ = z3.shape
    w_uv = jnp.swapaxes(w_uv, 1, 2)
    qb = DSA_QB
    return pl.pallas_call(
        functools.partial(_dsa_kernel, seq=seq),
        grid=(bsz, seq // qb),
        in_specs=[
            pl.BlockSpec((None, qb, B_WIDTH), lambda b, j: (b, j, COL_Q // B_WIDTH)),
            pl.BlockSpec((None, qb, B_WIDTH), lambda b, j: (b, j, COL_QI // B_WIDTH)),
            pl.BlockSpec((None, qb, LANES), lambda b, j: (b, j, COL_KW // LANES)),
            pl.BlockSpec((None, seq, KV_LATENT), lambda b, j: (b, 0, COL_CKV // KV_LATENT)),
            pl.BlockSpec((None, seq, LANES), lambda b, j: (b, 0, COL_KW // LANES)),
            pl.BlockSpec((1, KV_LATENT), lambda b, j: (0, 0)),
            pl.BlockSpec((1, LANES), lambda b, j: (0, 0)),
            pl.BlockSpec(w_uk.shape, lambda b, j: (0, 0, 0)),
            pl.BlockSpec(w_uv.shape, lambda b, j: (0, 0, 0)),
        ],
        out_specs=pl.BlockSpec((None, qb, B_WIDTH), lambda b, j: (b, j, 0)),
        out_shape=jax.ShapeDtypeStruct((bsz, seq, B_WIDTH), BF16),
        scratch_shapes=[
            pltpu.VMEM((seq, KV_LATENT), BF16),
            pltpu.VMEM((seq // DSA_TK, KV_LATENT, DSA_TK), BF16),
            pltpu.VMEM((seq, LANES), BF16),
            pltpu.VMEM((seq, LANES), BF16),
            pltpu.VMEM((seq // DSA_TK, DSA_TK, qb), jnp.int32),
            pltpu.VMEM((1, qb), jnp.int32),
            pltpu.VMEM((B_HEADS * qb, KV_LATENT), BF16),
            pltpu.VMEM((1, B_HEADS * qb), F32),
            pltpu.VMEM((1, B_HEADS * qb), F32),
            pltpu.VMEM((KV_LATENT, B_HEADS * qb), F32),
        ],
        compiler_params=pltpu.CompilerParams(
            dimension_semantics=("parallel", "arbitrary"), vmem_limit_bytes=VMEM_LIMIT),
        name="dsa",
    )(z3, z3, z3, z3, z3, kv_norm, ikn_pad, w_uk, w_uv)


def _pad_cols(w, width):
    return jnp.pad(w, ((0, 0), (0, width - w.shape[1])))


def _layer0_params(ab_w_in, mu, w0, w_up, a0, a_up, g_up, k_k, k_a, r_k, gn_g, gn_b, idx_k_norm):
    a_in = 3 * A_WIDTH + LORA_W + LORA_A + LORA_G
    o = 3 * A_WIDTH
    ab_w_in = ab_w_in.astype(BF16)
    w_r, w_k, w_v = ab_w_in[:, :A_WIDTH], ab_w_in[:, A_WIDTH:2 * A_WIDTH], ab_w_in[:, 2 * A_WIDTH:o]
    w_xwxa = ab_w_in[:, o:o + LORA_W + LORA_A]
    w_xg = ab_w_in[:, o + LORA_W + LORA_A:a_in]
    wb = ab_w_in[:, a_in:]
    o1 = B_WIDTH
    o2 = o1 + KV_LATENT
    o3 = o2 + IDX_HEADS * IDX_HEAD_DIM
    w_q, w_ckv, w_qi, w_kw = wb[:, :o1], wb[:, o1:o2], wb[:, o2:o3], wb[:, o3:]
    w_in = jnp.concatenate(
        [w_r, w_k, w_v, w_q, w_qi, w_ckv, _pad_cols(w_xg, 256), w_xwxa, _pad_cols(w_kw, LANES)], axis=1)
    w_in = _pad_cols(w_in, AB_IN_PAD)

    rows = [mu[:A_WIDTH], mu[A_WIDTH:2 * A_WIDTH], mu[2 * A_WIDTH:o], w0, a0, k_k, k_a,
            r_k.reshape(-1), gn_g, gn_b]
    pp = jnp.stack(rows + [jnp.zeros_like(w0)] * (RW_PP_ROWS - len(rows)), axis=0)
    mu_xx = mu[o:o + LORA_W + LORA_A][None, :]
    mu_xg = jnp.pad(mu[o + LORA_W + LORA_A:a_in], (0, 256 - LORA_G))[None, :]
    wup = jnp.pad(w_up, ((0, LANES - LORA_W), (0, 0))).astype(BF16)
    aup = jnp.pad(a_up, ((LORA_W, 0), (0, 0))).astype(BF16)
    gup = jnp.pad(g_up, ((0, 256 - LORA_G), (0, 0))).astype(BF16)
    ikn = jnp.pad(idx_k_norm, (0, LANES - IDX_HEAD_DIM))[None, :]
    return w_in, pp, mu_xx, mu_xg, wup, aup, gup, ikn


def kernel(x, norm_g, ab_w_in, ab_shift_mu, rw_w0, rw_w_up, rw_a0, rw_a_up, rw_g_up, rw_k_k, rw_k_a, rw_r_k, rw_gn_g, rw_gn_b, mla_kv_norm, mla_w_uk, mla_w_uv, idx_k_norm, ab_w_out, sg_w_in, sg_b_in, sg_ln_g, sg_ln_b, sg_w_s, sg_b_s, sg_w_out, ffn_w_up, ffn_conv_w, ffn_conv_b, ffn_w_down):
    bsz, seq, d = x.shape
    n = bsz * seq
    depth = norm_g.shape[0]
    h = x.reshape(n, d)
    ffn_w_up16 = ffn_w_up.astype(BF16)
    ffn_w_down16 = ffn_w_down.astype(BF16)
    for layer in range(depth):
        g_pre_mix, g_post_mix, g_pre_ffn, g_post_ffn = [norm_g[layer, i][None, :] for i in range(4)]
        if layer % 2 == 0:
            e = layer // 2
            w_in, pp, mu_xx, mu_xg, wup, aup, gup, ikn = _layer0_params(
                ab_w_in[e], ab_shift_mu[e], rw_w0[e], rw_w_up[e], rw_a0[e], rw_a_up[e], rw_g_up[e],
                rw_k_k[e], rw_k_a[e], rw_r_k[e], rw_gn_g[e], rw_gn_b[e], idx_k_norm[e])
            z = _norm_matmul(h, g_pre_mix, w_in, jnp.zeros((1, AB_IN_PAD), F32), gelu=False)
            z3 = z.reshape(bsz, seq, AB_IN_PAD)
            y_a = _rwkv(z3, pp, mu_xx, mu_xg, wup, aup, gup)
            y_b = _dsa(z3, mla_kv_norm[e][None, :], ikn, mla_w_uk[e].astype(BF16),
                       mla_w_uv[e].astype(BF16))
            h = _outproj(y_a.reshape(n, A_WIDTH), y_b.reshape(n, B_WIDTH),
                         ab_w_out[e].astype(BF16), h, g_post_mix)
        else:
            o = layer // 2
            z = _norm_matmul(h, g_pre_mix, sg_w_in[o].astype(BF16), sg_b_in[o][None, :], gelu=True)
            h = _sgu(z, sg_ln_g[o][None, :], sg_ln_b[o][None, :], sg_w_s[o], sg_b_s[o].T,
                     sg_w_out[o].astype(BF16), h, g_post_mix)
        h = _ffn(h, g_pre_ffn, g_post_ffn, ffn_w_up16, ffn_conv_w, ffn_conv_b[:, None, :], ffn_w_down16,
                 layer, seq=seq)
    return h.reshape(bsz, seq, d)
```

```python
import functools

import jax
import jax.numpy as jnp
from jax import lax
from jax.experimental import pallas as pl
from jax.experimental.pallas import tpu as pltpu

F32 = jnp.float32
BF16 = jnp.bfloat16

NORM_EPS = 1e-6
CHUNK = 64
A_HEADS = 16
A_HEAD_DIM = 64
A_WIDTH = A_HEADS * A_HEAD_DIM
LORA_W = 64
LORA_A = 64
LORA_G = 160
A_GN_EPS = 64e-5
B_HEADS = 8
B_HEAD_DIM = 128
B_WIDTH = B_HEADS * B_HEAD_DIM
KV_LATENT = 256
IDX_HEADS = 16
IDX_HEAD_DIM = 64
TOPK_MAX = 256
SG_CHUNK = 128
SG_GROUPS = 8
CONV_W = 3

LANES = 128
BF16_SUBLANES = 16
VMEM_LIMIT = 56 * 1024 * 1024

COL_R = 0
COL_K = A_WIDTH
COL_V = 2 * A_WIDTH
COL_Q = 3 * A_WIDTH
COL_QI = COL_Q + B_WIDTH
COL_CKV = COL_QI + IDX_HEADS * IDX_HEAD_DIM
COL_XG = COL_CKV + KV_LATENT
COL_XWXA = COL_XG + 256
COL_KW = COL_XWXA + LANES
AB_IN_PAD = 6144

RW_CHUNK = 64
NEG_BIG = -1e30


def _rms(x, g):
    ms = jnp.mean(x * x, axis=-1, keepdims=True)
    return x * lax.rsqrt(ms + NORM_EPS) * g


def _dot(a, b):
    return jnp.dot(a, b, preferred_element_type=F32)


def _dot_nt(a, b):
    return lax.dot_general(a, b, (((1,), (1,)), ((), ())), preferred_element_type=F32)


def _dot_tn(a, b):
    return lax.dot_general(a, b, (((0,), (0,)), ((), ())), preferred_element_type=F32)


def _split2(x):
    hi = x.astype(BF16)
    lo = (x - hi.astype(F32)).astype(BF16)
    return hi, lo


def _split3(x):
    hi = x.astype(BF16)
    r1 = x - hi.astype(F32)
    mid = r1.astype(BF16)
    lo = (r1 - mid.astype(F32)).astype(BF16)
    return hi, mid, lo


def _norm_mm_kernel(x_ref, g_ref, w_ref, b_ref, o_ref, xn_ref, *, gelu):
    @pl.when(pl.program_id(1) == 0)
    def _():
        xn_ref[...] = _rms(x_ref[...], g_ref[...]).astype(BF16)

    y = _dot(xn_ref[...], w_ref[...])
    if gelu:
        y = y + b_ref[...]
        y = 0.5 * y * (1.0 + lax.erf(y * 0.7071067811865476))
    o_ref[...] = y.astype(o_ref.dtype)


def _norm_matmul(x, g, w, b, *, gelu, tm=1024, tn=1024, out_dtype=F32):
    m, k = x.shape
    n = w.shape[1]
    return pl.pallas_call(
        functools.partial(_norm_mm_kernel, gelu=gelu),
        grid=(m // tm, n // tn),
        in_specs=[
            pl.BlockSpec((tm, k), lambda i, j: (i, 0)),
            pl.BlockSpec((1, k), lambda i, j: (0, 0)),
            pl.BlockSpec((k, tn), lambda i, j: (0, j)),
            pl.BlockSpec((1, tn), lambda i, j: (0, j)),
        ],
        out_specs=pl.BlockSpec((tm, tn), lambda i, j: (i, j)),
        out_shape=jax.ShapeDtypeStruct((m, n), out_dtype),
        scratch_shapes=[pltpu.VMEM((tm, k), BF16)],
        compiler_params=pltpu.CompilerParams(
            dimension_semantics=("parallel", "arbitrary"), vmem_limit_bytes=VMEM_LIMIT),
        name="norm_matmul",
    )(x, g, w, b)


FFN_HALO = BF16_SUBLANES


def _ffn_kernel(x_ref, xp_ref, gpre_ref, wg_ref, wu_ref, cwg_ref, cwu_ref, cbg_ref, cbu_ref,
                wd_ref, gpost_ref, o_ref, xn_ref, hg_ref, hu_ref, acc_ref, *, tm, tiles_per_seq):
    i = pl.program_id(0)
    j = pl.program_id(1)

    @pl.when(j == 0)
    def _():
        xn_ref[FFN_HALO:, :] = _rms(x_ref[...], gpre_ref[...]).astype(BF16)
        keep = jnp.where(i % tiles_per_seq != 0, 1.0, 0.0)
        xn_ref[:FFN_HALO, :] = (_rms(xp_ref[...], gpre_ref[...]) * keep).astype(BF16)
        acc_ref[...] = jnp.zeros_like(acc_ref)

    xn = xn_ref[...]
    hg_ref[...] = _dot(xn, wg_ref[...])
    hu_ref[...] = _dot(xn, wu_ref[...])

    def conv(h_ref, cw_ref, cb_ref):
        cw = cw_ref[...]
        return (h_ref[pl.ds(FFN_HALO, tm), :] * cw[2:3, :]
                + h_ref[pl.ds(FFN_HALO - 1, tm), :] * cw[1:2, :]
                + h_ref[pl.ds(FFN_HALO - 2, tm), :] * cw[0:1, :]
                + cb_ref[...])

    gate = conv(hg_ref, cwg_ref, cbg_ref)
    up = conv(hu_ref, cwu_ref, cbu_ref)
    act = gate * jax.nn.sigmoid(gate) * up
    acc_ref[...] += _dot(act.astype(BF16), wd_ref[...])

    @pl.when(j == pl.num_programs(1) - 1)
    def _():
        o_ref[...] = x_ref[...] + _rms(acc_ref[...], gpost_ref[...])


def _ffn(h, g_pre, g_post, w_up, conv_w, conv_b, w_down, layer, *, seq, tm=512, tf=512):
    m, d = h.shape
    d_ff = w_down.shape[1]
    nf = d_ff // tf
    halo_blocks = tm // FFN_HALO
    return pl.pallas_call(
        functools.partial(_ffn_kernel, tm=tm, tiles_per_seq=seq // tm),
        grid=(m // tm, nf),
        in_specs=[
            pl.BlockSpec((tm, d), lambda i, j: (i, 0)),
            pl.BlockSpec((FFN_HALO, d), lambda i, j: (jnp.maximum(i * halo_blocks - 1, 0), 0)),
            pl.BlockSpec((1, d), lambda i, j: (0, 0)),
            pl.BlockSpec((None, d, tf), lambda i, j: (layer, 0, j)),
            pl.BlockSpec((None, d, tf), lambda i, j: (layer, 0, j + nf)),
            pl.BlockSpec((None, CONV_W, tf), lambda i, j: (layer, 0, j)),
            pl.BlockSpec((None, CONV_W, tf), lambda i, j: (layer, 0, j + nf)),
            pl.BlockSpec((None, 1, tf), lambda i, j: (layer, 0, j)),
            pl.BlockSpec((None, 1, tf), lambda i, j: (layer, 0, j + nf)),
            pl.BlockSpec((None, tf, d), lambda i, j: (layer, j, 0)),
            pl.BlockSpec((1, d), lambda i, j: (0, 0)),
        ],
        out_specs=pl.BlockSpec((tm, d), lambda i, j: (i, 0)),
        out_shape=jax.ShapeDtypeStruct((m, d), F32),
        scratch_shapes=[
            pltpu.VMEM((tm + FFN_HALO, d), BF16),
            pltpu.VMEM((tm + FFN_HALO, tf), F32),
            pltpu.VMEM((tm + FFN_HALO, tf), F32),
            pltpu.VMEM((tm, d), F32),
        ],
        compiler_params=pltpu.CompilerParams(
            dimension_semantics=("parallel", "arbitrary"), vmem_limit_bytes=VMEM_LIMIT),
        name="conv_ffn",
    )(h, h, g_pre, w_up, w_up, conv_w, conv_w, conv_b, conv_b, w_down, g_post)


def _outproj_kernel(ya_ref, yb_ref, w_ref, h_ref, g_ref, o_ref):
    ka = ya_ref.shape[1]
    mix = _dot(ya_ref[...], w_ref[:ka, :]) + _dot(yb_ref[...], w_ref[ka:, :])
    o_ref[...] = h_ref[...] + _rms(mix, g_ref[...])


def _outproj(ya, yb, w, h, g, *, tm=512):
    m, d = h.shape
    return pl.pallas_call(
        _outproj_kernel,
        grid=(m // tm,),
        in_specs=[
            pl.BlockSpec((tm, ya.shape[1]), lambda i: (i, 0)),
            pl.BlockSpec((tm, yb.shape[1]), lambda i: (i, 0)),
            pl.BlockSpec(w.shape, lambda i: (0, 0)),
            pl.BlockSpec((tm, d), lambda i: (i, 0)),
            pl.BlockSpec((1, d), lambda i: (0, 0)),
        ],
        out_specs=pl.BlockSpec((tm, d), lambda i: (i, 0)),
        out_shape=jax.ShapeDtypeStruct((m, d), F32),
        compiler_params=pltpu.CompilerParams(
            dimension_semantics=("parallel",), vmem_limit_bytes=VMEM_LIMIT),
        name="outproj",
    )(ya, yb, w, h, g)


def _sgu_kernel(u_ref, v_ref, lng_ref, lnb_ref, ws_ref, bs_ref, wo_ref, h_ref, g_ref, o_ref,
                vln_ref, gated_ref, *, tm):
    v = v_ref[...]
    mu = jnp.mean(v, axis=-1, keepdims=True)
    dv = v - mu
    var = jnp.mean(dv * dv, axis=-1, keepdims=True)
    vln_ref[...] = (dv * lax.rsqrt(var + 1e-5) * lng_ref[...] + lnb_ref[...]).astype(BF16)

    gw = v.shape[1] // SG_GROUPS
    ii = lax.broadcasted_iota(jnp.int32, (SG_CHUNK, SG_CHUNK), 0)
    jj = lax.broadcasted_iota(jnp.int32, (SG_CHUNK, SG_CHUNK), 1)
    visible = (jj // CHUNK) <= (ii // CHUNK)
    for g in range(SG_GROUPS):
        wm = jnp.where(visible, ws_ref[g], 0.0).astype(BF16)
        bcol = bs_ref[:, g:g + 1]
        cols = slice(g * gw, (g + 1) * gw)
        for c in range(tm // SG_CHUNK):
            rows = slice(c * SG_CHUNK, (c + 1) * SG_CHUNK)
            s = _dot(wm, vln_ref[rows, cols]) + bcol
            gated_ref[rows, cols] = (u_ref[rows, cols] * s).astype(BF16)

    mix = _dot(gated_ref[...], wo_ref[...])
    o_ref[...] = h_ref[...] + _rms(mix, g_ref[...])


def _sgu(z, ln_g, ln_b, w_s, b_s_t, w_out, h, g, *, tm=256):
    m, d = h.shape
    return pl.pallas_call(
        functools.partial(_sgu_kernel, tm=tm),
        grid=(m // tm,),
        in_specs=[
            pl.BlockSpec((tm, d), lambda i: (i, 0)),
            pl.BlockSpec((tm, d), lambda i: (i, 1)),
            pl.BlockSpec((1, d), lambda i: (0, 0)),
            pl.BlockSpec((1, d), lambda i: (0, 0)),
            pl.BlockSpec(w_s.shape, lambda i: (0, 0, 0)),
            pl.BlockSpec(b_s_t.shape, lambda i: (0, 0)),
            pl.BlockSpec(w_out.shape, lambda i: (0, 0)),
            pl.BlockSpec((tm, d), lambda i: (i, 0)),
            pl.BlockSpec((1, d), lambda i: (0, 0)),
        ],
        out_specs=pl.BlockSpec((tm, d), lambda i: (i, 0)),
        out_shape=jax.ShapeDtypeStruct((m, d), F32),
        scratch_shapes=[pltpu.VMEM((tm, d), BF16), pltpu.VMEM((tm, d), BF16)],
        compiler_params=pltpu.CompilerParams(
            dimension_semantics=("parallel",), vmem_limit_bytes=VMEM_LIMIT),
        name="sgu",
    )(z, z, ln_g, ln_b, w_s, b_s_t, w_out, h, g)


RW_PP_ROWS = 16
(PP_MU_R, PP_MU_K, PP_MU_V, PP_W0, PP_A0, PP_KK, PP_KA, PP_RK, PP_GNG, PP_GNB) = range(10)
RW_CARRY = 8
RW_GROUP = 8


def _rwkv_kernel(r_ref, k_ref, v_ref, xx_ref, xg_ref, pp_ref, mux_ref, mug_ref, wup_ref, aup_ref,
                 gup_ref, y_ref,
                 rb_ref, kb_ref, vb_ref, xb_ref, gb_ref,
                 r_s, k_s, v_s, lw_s, kk_s, be_s, g_s, bonus_s, y_s, ra_s, z_s, phi_s, psi_s, state_ref,
                 *, tt):
    L = RW_CHUNK
    tb = pl.program_id(2)
    pp = pp_ref[...]
    row = lambda i: pp[i:i + 1, :]

    lane = lax.broadcasted_iota(jnp.int32, (1, LANES), 1)
    head_masks = (lane < A_HEAD_DIM, lane >= A_HEAD_DIM)
    li = lax.broadcasted_iota(jnp.int32, (LANES, LANES), 0)
    lj = lax.broadcasted_iota(jnp.int32, (LANES, LANES), 1)
    same_head = (li // A_HEAD_DIM) == (lj // A_HEAD_DIM)

    def seg_sum(x):
        s0 = jnp.sum(jnp.where(head_masks[0], x, 0.0), axis=-1, keepdims=True)
        s1 = jnp.sum(jnp.where(head_masks[1], x, 0.0), axis=-1, keepdims=True)
        return jnp.where(head_masks[0], s0, s1)

    @pl.when(tb == 0)
    def _():
        for b in (rb_ref, kb_ref, vb_ref, xb_ref, gb_ref):
            b[0:RW_CARRY, :] = jnp.zeros((RW_CARRY, b.shape[1]), F32)
        state_ref[...] = jnp.zeros_like(state_ref)

    def shift_mix(x_ref, buf_ref, mu):
        x = x_ref[...]
        buf_ref[RW_CARRY:, :] = x
        prev = buf_ref[pl.ds(RW_CARRY - 1, tt), :]
        return x + (prev - x) * mu

    r = shift_mix(r_ref, rb_ref, row(PP_MU_R))
    k = shift_mix(k_ref, kb_ref, row(PP_MU_K))
    v = shift_mix(v_ref, vb_ref, row(PP_MU_V))
    xx = shift_mix(xx_ref, xb_ref, mux_ref[...])
    xg = shift_mix(xg_ref, gb_ref, mug_ref[...])
    for src, buf in ((r_ref, rb_ref), (k_ref, kb_ref), (v_ref, vb_ref), (xx_ref, xb_ref),
                     (xg_ref, gb_ref)):
        buf[0:RW_CARRY, :] = src[tt - RW_CARRY:tt, :]

    w_pre = row(PP_W0) + _dot(jnp.tanh(xx).astype(BF16), wup_ref[...])
    log_w = -(jnp.maximum(-w_pre, 0.0) + jnp.log1p(jnp.exp(-jnp.abs(w_pre)))) - 0.5
    lw_s[...] = -jnp.exp(log_w)
    a = jax.nn.sigmoid(row(PP_A0) + _dot(xx.astype(BF16), aup_ref[...]))
    g_s[...] = _dot(jax.nn.sigmoid(xg).astype(BF16), gup_ref[...])
    kk = k * row(PP_KK)
    kk = kk * lax.rsqrt(seg_sum(kk * kk) + 1e-12)
    k = k * (1.0 + (a - 1.0) * row(PP_KA))
    bonus_s[...] = seg_sum(r * k * row(PP_RK)) * v
    r_s[...] = r
    k_s[...] = k
    v_s[...] = v
    kk_s[...] = kk
    be_s[...] = kk * a

    tl = li % L
    cl = lj % L
    strict_bd = same_head & (cl < tl)
    incl_bd = same_head & (cl <= tl)
    tri_bd = jnp.where(incl_bd, 1.0, 0.0).astype(BF16)
    eye = jnp.where(li == lj, 1.0, 0.0).astype(F32)

    def stack(x):
        return jnp.concatenate([jnp.where(head_masks[0], x, 0.0), jnp.where(head_masks[1], x, 0.0)],
                               axis=0)

    state = [state_ref[...]]
    pending = []

    def tick():
        if pending:
            pending.pop(0)()

    def make_step(c):
        def step():
            s_hi, s_lo = _split2(state[0])
            y_st = _dot_nt(ra_s[c].astype(BF16), s_hi) + z_s[c]
            y_s[c * L:(c + 1) * L, :] = y_st[:L, :] + y_st[L:, :]
            p_hi, p_lo = _split2(phi_s[c])
            state[0] = _dot(s_hi, p_hi) + _dot(s_lo, p_hi) + _dot(s_hi, p_lo) + psi_s[c]
        return step

    def chunk_group(gi):
        G = range(RW_GROUP)
        cs, lw = [], []
        for half in range(RW_GROUP // 2):
            c0 = gi * RW_GROUP + 2 * half
            lw2 = lw_s[c0 * L:(c0 + 2) * L, :]
            h3 = _split3(lw2)
            cs2 = _dot(tri_bd, h3[0]) + _dot(tri_bd, h3[1]) + _dot(tri_bd, h3[2])
            cs += [cs2[:L, :], cs2[L:, :]]
            lw += [lw2[:L, :], lw2[L:, :]]
        tick()
        cidx = [gi * RW_GROUP + i for i in G]
        sls = [slice(c * L, (c + 1) * L) for c in cidx]
        e_pos = [jnp.exp(cs[i]) for i in G]
        e_prev = [jnp.exp(cs[i] - lw[i]) for i in G]
        e_neg = [jnp.exp(-cs[i]) for i in G]
        lam_l = [e_pos[i][L - 1:L, :] for i in G]
        rb_f = [stack(r_s[sls[i], :] * e_pos[i]) for i in G]
        ab = [stack(-kk_s[sls[i], :] * e_prev[i]).astype(BF16) for i in G]
        rb = [rb_f[i].astype(BF16) for i in G]
        bt = [stack(be_s[sls[i], :] * e_neg[i]).astype(BF16) for i in G]
        kt = [stack(k_s[sls[i], :] * e_neg[i]).astype(BF16) for i in G]
        v16 = [stack(v_s[sls[i], :]).astype(BF16) for i in G]
        tick()
        g = [_dot_nt(jnp.concatenate([ab[i], rb[i]], axis=0), jnp.concatenate([bt[i], kt[i]], axis=0))
             for i in G]
        a_ab = [jnp.where(strict_bd, g[i][:LANES, :LANES], 0.0) for i in G]
        a_ak = [jnp.where(strict_bd, g[i][:LANES, LANES:], 0.0).astype(BF16) for i in G]
        a_rb = [jnp.where(incl_bd, g[i][LANES:, :LANES], 0.0).astype(BF16) for i in G]
        a_rk = [jnp.where(incl_bd, g[i][LANES:, LANES:], 0.0).astype(BF16) for i in G]
        tick()
        t = [eye + a_ab[i] for i in G]
        p = a_ab
        for _ in range(5):
            p16 = [p[i].astype(BF16) for i in G]
            p = [_dot(p16[i], p16[i]) for i in G]
            tick()
            t = [t[i] + _dot(t[i].astype(BF16), p[i].astype(BF16)) for i in G]
            tick()
        t16 = [t[i].astype(BF16) for i in G]
        av = [_dot(a_ak[i], v16[i]).astype(BF16) for i in G]
        ta = [_dot(t16[i], ab[i]).astype(BF16) for i in G]
        tick()
        w16 = [_dot(t16[i], av[i]).astype(BF16) for i in G]
        for i in G:
            ra_s[cidx[i]] = rb_f[i] + _dot(a_rb[i], ta[i])
        tick()
        for i in G:
            z_s[cidx[i]] = _dot(a_rb[i], w16[i]) + _dot(a_rk[i], v16[i])
        tick()
        for i in G:
            phi_s[cidx[i]] = (eye + _dot_tn(ta[i], bt[i])) * lam_l[i]
        tick()
        for i in G:
            psi_s[cidx[i]] = (_dot_tn(w16[i], bt[i]) + _dot_tn(v16[i], kt[i])) * lam_l[i]
        while pending:
            tick()
        pending.extend(make_step(c) for c in cidx)

    for gi in range(tt // (L * RW_GROUP)):
        chunk_group(gi)
    while pending:
        tick()
    state_ref[...] = state[0]

    y = y_s[...]
    inv_n = 1.0 / A_HEAD_DIM
    mean = seg_sum(y) * inv_n
    d = y - mean
    var = seg_sum(d * d) * inv_n
    yn = d * lax.rsqrt(var + A_GN_EPS) * row(PP_GNG) + row(PP_GNB)
    y_ref[...] = ((yn + bonus_s[...]) * g_s[...]).astype(y_ref.dtype)


def _rwkv(z3, pp, mu_xx, mu_xg, wup, aup, gup, *, tt=1024):
    bsz, seq, _ = z3.shape
    pairs = A_WIDTH // LANES
    cb = lambda col, w: col // w
    col_spec = lambda col0: pl.BlockSpec((None, tt, LANES), lambda b, p, t: (b, t, cb(col0, LANES) + p))
    buf = lambda w: pltpu.VMEM((tt + RW_CARRY, w), F32)
    big = pltpu.VMEM((tt, LANES), F32)
    return pl.pallas_call(
        functools.partial(_rwkv_kernel, tt=tt),
        grid=(bsz, pairs, seq // tt),
        in_specs=[
            col_spec(COL_R), col_spec(COL_K), col_spec(COL_V),
            pl.BlockSpec((None, tt, LANES), lambda b, p, t: (b, t, cb(COL_XWXA, LANES))),
            pl.BlockSpec((None, tt, 256), lambda b, p, t: (b, t, cb(COL_XG, 256))),
            pl.BlockSpec((RW_PP_ROWS, LANES), lambda b, p, t: (0, p)),
            pl.BlockSpec((1, LANES), lambda b, p, t: (0, 0)),
            pl.BlockSpec((1, 256), lambda b, p, t: (0, 0)),
            pl.BlockSpec((LANES, LANES), lambda b, p, t: (0, p)),
            pl.BlockSpec((LANES, LANES), lambda b, p, t: (0, p)),
            pl.BlockSpec((256, LANES), lambda b, p, t: (0, p)),
        ],
        out_specs=pl.BlockSpec((None, tt, LANES), lambda b, p, t: (b, t, p)),
        out_shape=jax.ShapeDtypeStruct((bsz, seq, A_WIDTH), BF16),
        scratch_shapes=[buf(LANES), buf(LANES), buf(LANES), buf(LANES), buf(256)]
        + [big] * 9 + [pltpu.VMEM((tt // RW_CHUNK, LANES, LANES), F32)] * 4
        + [pltpu.VMEM((LANES, LANES), F32)],
        compiler_params=pltpu.CompilerParams(
            dimension_semantics=("parallel", "parallel", "arbitrary"), vmem_limit_bytes=VMEM_LIMIT),
        name="rwkv7",
    )(z3, z3, z3, z3, z3, pp, mu_xx, mu_xg, wup, aup, gup)


DSA_TK = 512
DSA_QB = 256
DSA_HG = 1
LOG2E = 1.4426950408889634


def _dsa_kernel(q_ref, qi_ref, kwq_ref, ckv_ref, kwf_ref, kvn_ref, ikn_ref, wuk_ref, wuvt_ref,
                o_ref, ckv_s, ckvt_s, ke_s, ko_s, key_s, key_hi_s, key_lo_s, cut_s, qlat_s, m_s, l_s,
                acc_s, *, seq):
    j = pl.program_id(1)
    tk = DSA_TK
    qb = DSA_QB
    lane = lax.broadcasted_iota(jnp.int32, (1, LANES), 1)
    qlane = lax.broadcasted_iota(jnp.int32, (1, qb), 1)

    @pl.when(j == 0)
    def _():
        def prep(t, carry):
            sl = pl.ds(pl.multiple_of(t * tk, tk), tk)
            ckv = _rms(ckv_ref[sl, :], kvn_ref[...])
            ckv_s[sl, :] = ckv.astype(BF16)
            ckvt_s[t] = ckv.T.astype(BF16)
            kw = kwf_ref[sl, :]
            ksq = jnp.where(lane < IDX_HEAD_DIM, kw * kw, 0.0)
            ms = jnp.sum(ksq, axis=-1, keepdims=True) * (1.0 / IDX_HEAD_DIM)
            kn = kw * lax.rsqrt(ms + NORM_EPS) * ikn_ref[...]
            ke_s[sl, :] = kn.astype(BF16)
            ko_s[sl, :] = pltpu.roll(kn, IDX_HEAD_DIM, axis=1).astype(BF16)
            return carry
        lax.fori_loop(0, seq // tk, prep, 0)

    n_tiles = (j * qb + qb + tk - 1) // tk
    limit = j * qb + (qlane // CHUNK + 1) * CHUNK
    krow = lax.broadcasted_iota(jnp.int32, (tk, 1), 0)

    w_t = kwq_ref[...].T * ((IDX_HEADS ** -0.5) * (IDX_HEAD_DIM ** -0.5))
    w_rows = [w_t[IDX_HEAD_DIM + h:IDX_HEAD_DIM + h + 1, :] for h in range(IDX_HEADS)]
    qi16 = [qi_ref[:, p * LANES:(p + 1) * LANES].astype(BF16) for p in range(IDX_HEADS // 2)]

    def score_tile(t, carry):
        sl = pl.ds(pl.multiple_of(t * tk, tk), tk)
        ke = ke_s[sl, :]
        ko = ko_s[sl, :]
        sc = jnp.zeros((tk, qb), F32)
        for p in range(IDX_HEADS // 2):
            xe = jnp.maximum(_dot_nt(ke, qi16[p]), 0.0)
            xo = jnp.maximum(_dot_nt(ko, qi16[p]), 0.0)
            sc = sc + (xe * w_rows[2 * p] + xo * w_rows[2 * p + 1])
        sc = jnp.where(t * tk + krow < limit, sc, -jnp.inf)
        bits = pltpu.bitcast(sc, jnp.int32)
        key = bits ^ ((bits >> 31) & 0x7FFFFFFF)
        key_s[t] = key
        key_hi_s[t] = (key >> 16).astype(jnp.int16)
        key_lo_s[t] = ((key & 0xFFFF) - 0x8000).astype(jnp.int16)
        return carry

    lax.fori_loop(0, n_tiles, score_tile, 0)

    rows16 = BF16_SUBLANES
    n_acc = 4

    def half16(x):
        return jnp.broadcast_to(x, (rows16, qb)).astype(jnp.int16)

    def count_ge16(src_s, c16):
        one = jnp.ones((), BF16)
        zero = jnp.zeros((), BF16)

        def body(t, accs):
            accs = list(accs)
            for i in range(tk // rows16):
                hit = jnp.where(src_s[t, i * rows16:(i + 1) * rows16, :] >= c16, one, zero)
                accs[i % n_acc] = accs[i % n_acc] + hit
            return tuple(accs)

        accs = lax.fori_loop(0, n_tiles, body, tuple(jnp.zeros((rows16, qb), BF16) for _ in range(n_acc)))
        total = (accs[0] + accs[1]) + (accs[2] + accs[3])
        return jnp.sum(total.astype(F32), axis=0, keepdims=True)

    def count_ge_hi(cand):
        return count_ge16(key_hi_s, half16(cand >> 16))

    def count_ge_lo(cand):
        return count_ge16(key_lo_s, half16((cand & 0xFFFF) - 0x8000))

    def count_where(pred):
        def body(t, part):
            hit = jnp.where(pred(key_s[t], t), 1.0, 0.0)
            return part + jnp.sum(hit.reshape(4, tk // 32, 8, qb), axis=1)
        part = lax.fori_loop(0, n_tiles, body, jnp.zeros((4, 8, qb), F32))
        return jnp.sum(jnp.sum(part, axis=0), axis=0, keepdims=True)

    def count_ge(cand):
        return count_where(lambda k, t: k >= cand)

    int_min = jnp.int32(-2 ** 31)
    topk = float(TOPK_MAX)
    theta = jnp.where(count_ge_hi(jnp.zeros((1, qb), jnp.int32)) >= topk, 0, int_min)

    def hi_step(i, theta):
        cand = theta | (jnp.int32(1) << (30 - i))
        return jnp.where(count_ge_hi(cand) >= topk, cand, theta)

    def lo_step(i, theta):
        cand = theta | (jnp.int32(1) << (15 - i))
        return jnp.where(count_ge_lo(cand) >= topk, cand, theta)

    theta = lax.fori_loop(0, 15, hi_step, theta)

    th_hi = half16(theta >> 16)

    def merge_halves(t, carry):
        for i in range(tk // rows16):
            rows = slice(i * rows16, (i + 1) * rows16)
            hi = key_hi_s[t, rows, :]
            key_lo_s[t, rows, :] = jnp.where(
                hi > th_hi, jnp.int16(0x7FFF),
                jnp.where(hi == th_hi, key_lo_s[t, rows, :], jnp.int16(-0x8000)))
        return carry

    lax.fori_loop(0, n_tiles, merge_halves, 0)
    theta = lax.fori_loop(0, 16, lo_step, theta)

    seq_bits = (seq - 1).bit_length()
    cut_s[...] = jnp.full_like(cut_s, seq)
    n_ge = count_ge(theta)

    @pl.when(jnp.max(n_ge) > topk)
    def _():
        n_gt = n_ge - count_where(lambda k, t: k == theta)
        keep = topk - n_gt

        def idx_step(i, cut):
            cand = cut | (jnp.int32(1) << (seq_bits - 1 - i))
            below = count_where(lambda k, t: (k == theta) & (t * tk + krow < cand))
            return jnp.where(below <= keep - 1.0, cand, cut)

        cut_s[...] = lax.fori_loop(0, seq_bits, idx_step, jnp.zeros((1, qb), jnp.int32))

    cut = cut_s[...]

    for h in range(B_HEADS):
        qh = q_ref[:, h * B_HEAD_DIM:(h + 1) * B_HEAD_DIM].astype(BF16)
        qlat_s[h * qb:(h + 1) * qb, :] = (
            _dot(qh, wuk_ref[h]) * ((B_HEAD_DIM ** -0.5) * LOG2E)).astype(BF16)

    m_s[...] = jnp.full_like(m_s, NEG_BIG)
    l_s[...] = jnp.zeros_like(l_s)
    acc_s[...] = jnp.zeros_like(acc_s)
    gw = DSA_HG * qb
    n_groups = B_HEADS // DSA_HG

    def attend(t, carry):
        ckv = ckv_s[pl.ds(pl.multiple_of(t * tk, tk), tk), :]
        ckvt = ckvt_s[t]
        keys = key_s[t]
        kpos = t * tk + krow
        sel = ((keys > theta) | ((keys == theta) & (kpos <= cut))) & (kpos < limit)
        bias1 = jnp.where(sel, 0.0, NEG_BIG)
        bias = bias1 if DSA_HG == 1 else jnp.concatenate([bias1] * DSA_HG, axis=1)
        lgs = [_dot_nt(ckv, qlat_s[g * gw:(g + 1) * gw, :]) for g in range(n_groups)]
        for g in range(n_groups):
            lg = lgs[g] + bias
            cols = slice(g * gw, (g + 1) * gw)
            m_old = m_s[:, cols]
            m_new = jnp.maximum(m_old, jnp.max(lg, axis=0, keepdims=True))
            alpha = jnp.exp2(m_old - m_new)
            p = jnp.exp2(lg - m_new)
            l_s[:, cols] = alpha * l_s[:, cols] + jnp.sum(p, axis=0, keepdims=True)
            acc_s[:, cols] = alpha * acc_s[:, cols] + _dot(ckvt, p.astype(BF16))
            m_s[:, cols] = m_new
        return carry

    lax.fori_loop(0, n_tiles, attend, 0)

    for h in range(B_HEADS):
        cols = slice(h * qb, (h + 1) * qb)
        o_t = (acc_s[:, cols] * (1.0 / l_s[:, cols])).astype(BF16)
        o_ref[:, h * B_HEAD_DIM:(h + 1) * B_HEAD_DIM] = _dot(wuvt_ref[h], o_t).T.astype(o_ref.dtype)


def _dsa(z3, kv_norm, ikn_pad, w_uk, w_uv):
    bsz, seq, _ = z3.shape
    w_uv = jnp.swapaxes(w_uv, 1, 2)
    qb = DSA_QB
    assert seq // BF16_SUBLANES <= 256, "bf16 partial counts of the top-k search must stay exact"
    return pl.pallas_call(
        functools.partial(_dsa_kernel, seq=seq),
        grid=(bsz, seq // qb),
        in_specs=[
            pl.BlockSpec((None, qb, B_WIDTH), lambda b, j: (b, j, COL_Q // B_WIDTH)),
            pl.BlockSpec((None, qb, B_WIDTH), lambda b, j: (b, j, COL_QI // B_WIDTH)),
            pl.BlockSpec((None, qb, LANES), lambda b, j: (b, j, COL_KW // LANES)),
            pl.BlockSpec((None, seq, KV_LATENT), lambda b, j: (b, 0, COL_CKV // KV_LATENT)),
            pl.BlockSpec((None, seq, LANES), lambda b, j: (b, 0, COL_KW // LANES)),
            pl.BlockSpec((1, KV_LATENT), lambda b, j: (0, 0)),
            pl.BlockSpec((1, LANES), lambda b, j: (0, 0)),
            pl.BlockSpec(w_uk.shape, lambda b, j: (0, 0, 0)),
            pl.BlockSpec(w_uv.shape, lambda b, j: (0, 0, 0)),
        ],
        out_specs=pl.BlockSpec((None, qb, B_WIDTH), lambda b, j: (b, j, 0)),
        out_shape=jax.ShapeDtypeStruct((bsz, seq, B_WIDTH), BF16),
        scratch_shapes=[
            pltpu.VMEM((seq, KV_LATENT), BF16),
            pltpu.VMEM((seq // DSA_TK, KV_LATENT, DSA_TK), BF16),
            pltpu.VMEM((seq, LANES), BF16),
            pltpu.VMEM((seq, LANES), BF16),
            pltpu.VMEM((seq // DSA_TK, DSA_TK, qb), jnp.int32),
            pltpu.VMEM((seq // DSA_TK, DSA_TK, qb), jnp.int16),
            pltpu.VMEM((seq // DSA_TK, DSA_TK, qb), jnp.int16),
            pltpu.VMEM((1, qb), jnp.int32),
            pltpu.VMEM((B_HEADS * qb, KV_LATENT), BF16),
            pltpu.VMEM((1, B_HEADS * qb), F32),
            pltpu.VMEM((1, B_HEADS * qb), F32),
            pltpu.VMEM((KV_LATENT, B_HEADS * qb), F32),
        ],
        compiler_params=pltpu.CompilerParams(
            dimension_semantics=("parallel", "arbitrary"), vmem_limit_bytes=VMEM_LIMIT),
        name="dsa",
    )(z3, z3, z3, z3, z3, kv_norm, ikn_pad, w_uk, w_uv)


def _layer0_params(ab_w_in, mu, w0, w_up, a0, a_up, g_up, k_k, k_a, r_k, gn_g, gn_b, idx_k_norm):
    a_in = 3 * A_WIDTH + LORA_W + LORA_A + LORA_G
    o = 3 * A_WIDTH
    ab_w_in = ab_w_in.astype(BF16)
    w_r, w_k, w_v = ab_w_in[:, :A_WIDTH], ab_w_in[:, A_WIDTH:2 * A_WIDTH], ab_w_in[:, 2 * A_WIDTH:o]
    w_xwxa = ab_w_in[:, o:o + LORA_W + LORA_A]
    w_xg = ab_w_in[:, o + LORA_W + LORA_A:a_in]
    wb = ab_w_in[:, a_in:]
    o1 = B_WIDTH
    o2 = o1 + KV_LATENT
    o3 = o2 + IDX_HEADS * IDX_HEAD_DIM
    w_q, w_ckv, w_qi, w_kw = wb[:, :o1], wb[:, o1:o2], wb[:, o2:o3], wb[:, o3:]
    zeros = lambda w: jnp.zeros((ab_w_in.shape[0], w), BF16)
    w_in = jnp.concatenate(
        [w_r, w_k, w_v, w_q, w_qi, w_ckv, w_xg, zeros(256 - LORA_G), w_xwxa,
         w_kw, zeros(LANES - w_kw.shape[1]), zeros(AB_IN_PAD - COL_KW - LANES)], axis=1)

    rows = [mu[:A_WIDTH], mu[A_WIDTH:2 * A_WIDTH], mu[2 * A_WIDTH:o], w0, a0, k_k, k_a,
            r_k.reshape(-1), gn_g, gn_b]
    pp = jnp.stack(rows + [jnp.zeros_like(w0)] * (RW_PP_ROWS - len(rows)), axis=0)
    mu_xx = mu[o:o + LORA_W + LORA_A][None, :]
    mu_xg = jnp.pad(mu[o + LORA_W + LORA_A:a_in], (0, 256 - LORA_G))[None, :]
    wup = jnp.pad(w_up, ((0, LANES - LORA_W), (0, 0))).astype(BF16)
    aup = jnp.pad(a_up, ((LORA_W, 0), (0, 0))).astype(BF16)
    gup = jnp.pad(g_up, ((0, 256 - LORA_G), (0, 0))).astype(BF16)
    ikn = jnp.pad(idx_k_norm, (0, LANES - IDX_HEAD_DIM))[None, :]
    return w_in, pp, mu_xx, mu_xg, wup, aup, gup, ikn


def kernel(x, norm_g, ab_w_in, ab_shift_mu, rw_w0, rw_w_up, rw_a0, rw_a_up, rw_g_up, rw_k_k, rw_k_a, rw_r_k, rw_gn_g, rw_gn_b, mla_kv_norm, mla_w_uk, mla_w_uv, idx_k_norm, ab_w_out, sg_w_in, sg_b_in, sg_ln_g, sg_ln_b, sg_w_s, sg_b_s, sg_w_out, ffn_w_up, ffn_conv_w, ffn_conv_b, ffn_w_down):
    bsz, seq, d = x.shape
    n = bsz * seq
    depth = norm_g.shape[0]
    h = x.reshape(n, d)
    ffn_w_up16 = ffn_w_up.astype(BF16)
    ffn_w_down16 = ffn_w_down.astype(BF16)
    for layer in range(depth):
        g_pre_mix, g_post_mix, g_pre_ffn, g_post_ffn = [norm_g[layer, i][None, :] for i in range(4)]
        if layer % 2 == 0:
            e = layer // 2
            w_in, pp, mu_xx, mu_xg, wup, aup, gup, ikn = _layer0_params(
                ab_w_in[e], ab_shift_mu[e], rw_w0[e], rw_w_up[e], rw_a0[e], rw_a_up[e], rw_g_up[e],
                rw_k_k[e], rw_k_a[e], rw_r_k[e], rw_gn_g[e], rw_gn_b[e], idx_k_norm[e])
            z = _norm_matmul(h, g_pre_mix, w_in, jnp.zeros((1, AB_IN_PAD), F32), gelu=False)
            z3 = z.reshape(bsz, seq, AB_IN_PAD)
            y_a = _rwkv(z3, pp, mu_xx, mu_xg, wup, aup, gup)
            y_b = _dsa(z3, mla_kv_norm[e][None, :], ikn, mla_w_uk[e].astype(BF16),
                       mla_w_uv[e].astype(BF16))
            h = _outproj(y_a.reshape(n, A_WIDTH), y_b.reshape(n, B_WIDTH),
                         ab_w_out[e].astype(BF16), h, g_post_mix)
        else:
            o = layer // 2
            z = _norm_matmul(h, g_pre_mix, sg_w_in[o].astype(BF16), sg_b_in[o][None, :], gelu=True)
            h = _sgu(z, sg_ln_g[o][None, :], sg_ln_b[o][None, :], sg_w_s[o], sg_b_s[o].T,
                     sg_w_out[o].astype(BF16), h, g_post_mix)
        h = _ffn(h, g_pre_ffn, g_post_ffn, ffn_w_up16, ffn_conv_w, ffn_conv_b[:, None, :], ffn_w_down16,
                 layer, seq=seq)
    return h.reshape(bsz, seq, d)
```

```python
import functools

import jax
import jax.numpy as jnp
from jax import lax
from jax.experimental import pallas as pl
from jax.experimental.pallas import tpu as pltpu

F32 = jnp.float32
BF16 = jnp.bfloat16

NORM_EPS = 1e-6
CHUNK = 64
A_HEADS = 16
A_HEAD_DIM = 64
A_WIDTH = A_HEADS * A_HEAD_DIM
LORA_W = 64
LORA_A = 64
LORA_G = 160
A_GN_EPS = 64e-5
B_HEADS = 8
B_HEAD_DIM = 128
B_WIDTH = B_HEADS * B_HEAD_DIM
KV_LATENT = 256
IDX_HEADS = 16
IDX_HEAD_DIM = 64
TOPK_MAX = 256
SG_CHUNK = 128
SG_GROUPS = 8
CONV_W = 3

LANES = 128
BF16_SUBLANES = 16
VMEM_LIMIT = 56 * 1024 * 1024

COL_R = 0
COL_K = A_WIDTH
COL_V = 2 * A_WIDTH
COL_Q = 3 * A_WIDTH
COL_QI = COL_Q + B_WIDTH
COL_CKV = COL_QI + IDX_HEADS * IDX_HEAD_DIM
COL_XG = COL_CKV + KV_LATENT
COL_XWXA = COL_XG + 256
COL_KW = COL_XWXA + LANES
AB_IN_PAD = 6144

RW_CHUNK = 64
NEG_BIG = -1e30


def _rms(x, g):
    ms = jnp.mean(x * x, axis=-1, keepdims=True)
    return x * lax.rsqrt(ms + NORM_EPS) * g


def _dot(a, b):
    return jnp.dot(a, b, preferred_element_type=F32)


def _dot_nt(a, b):
    return lax.dot_general(a, b, (((1,), (1,)), ((), ())), preferred_element_type=F32)


def _dot_tn(a, b):
    return lax.dot_general(a, b, (((0,), (0,)), ((), ())), preferred_element_type=F32)


def _split2(x):
    hi = x.astype(BF16)
    lo = (x - hi.astype(F32)).astype(BF16)
    return hi, lo


def _split3(x):
    hi = x.astype(BF16)
    r1 = x - hi.astype(F32)
    mid = r1.astype(BF16)
    lo = (r1 - mid.astype(F32)).astype(BF16)
    return hi, mid, lo


def _norm_mm_kernel(x_ref, g_ref, w_ref, b_ref, o_ref, xn_ref, *, gelu):
    @pl.when(pl.program_id(1) == 0)
    def _():
        xn_ref[...] = _rms(x_ref[...], g_ref[...]).astype(BF16)

    y = _dot(xn_ref[...], w_ref[...])
    if gelu:
        y = y + b_ref[...]
        y = 0.5 * y * (1.0 + lax.erf(y * 0.7071067811865476))
    o_ref[...] = y.astype(o_ref.dtype)


def _norm_matmul(x, g, w, b, *, gelu, tm=1024, tn=1024, out_dtype=F32):
    m, k = x.shape
    n = w.shape[1]
    return pl.pallas_call(
        functools.partial(_norm_mm_kernel, gelu=gelu),
        grid=(m // tm, n // tn),
        in_specs=[
            pl.BlockSpec((tm, k), lambda i, j: (i, 0)),
            pl.BlockSpec((1, k), lambda i, j: (0, 0)),
            pl.BlockSpec((k, tn), lambda i, j: (0, j)),
            pl.BlockSpec((1, tn), lambda i, j: (0, j)),
        ],
        out_specs=pl.BlockSpec((tm, tn), lambda i, j: (i, j)),
        out_shape=jax.ShapeDtypeStruct((m, n), out_dtype),
        scratch_shapes=[pltpu.VMEM((tm, k), BF16)],
        compiler_params=pltpu.CompilerParams(
            dimension_semantics=("parallel", "arbitrary"), vmem_limit_bytes=VMEM_LIMIT),
        name="norm_matmul",
    )(x, g, w, b)


FFN_HALO = BF16_SUBLANES


def _ffn_kernel(x_ref, xp_ref, gpre_ref, wg_ref, wu_ref, cwg_ref, cwu_ref, cbg_ref, cbu_ref,
                wd_ref, gpost_ref, o_ref, xn_ref, hg_ref, hu_ref, acc_ref, *, tm, tiles_per_seq):
    i = pl.program_id(0)
    j = pl.program_id(1)

    @pl.when(j == 0)
    def _():
        xn_ref[FFN_HALO:, :] = _rms(x_ref[...], gpre_ref[...]).astype(BF16)
        keep = jnp.where(i % tiles_per_seq != 0, 1.0, 0.0)
        xn_ref[:FFN_HALO, :] = (_rms(xp_ref[...], gpre_ref[...]) * keep).astype(BF16)
        acc_ref[...] = jnp.zeros_like(acc_ref)

    xn = xn_ref[...]
    hg_ref[...] = _dot(xn, wg_ref[...])
    hu_ref[...] = _dot(xn, wu_ref[...])

    def conv(h_ref, cw_ref, cb_ref):
        cw = cw_ref[...]
        return (h_ref[pl.ds(FFN_HALO, tm), :] * cw[2:3, :]
                + h_ref[pl.ds(FFN_HALO - 1, tm), :] * cw[1:2, :]
                + h_ref[pl.ds(FFN_HALO - 2, tm), :] * cw[0:1, :]
                + cb_ref[...])

    gate = conv(hg_ref, cwg_ref, cbg_ref)
    up = conv(hu_ref, cwu_ref, cbu_ref)
    act = gate * jax.nn.sigmoid(gate) * up
    acc_ref[...] += _dot(act.astype(BF16), wd_ref[...])

    @pl.when(j == pl.num_programs(1) - 1)
    def _():
        o_ref[...] = x_ref[...] + _rms(acc_ref[...], gpost_ref[...])


def _ffn(h, g_pre, g_post, w_up, conv_w, conv_b, w_down, layer, *, seq, tm=512, tf=512):
    m, d = h.shape
    d_ff = w_down.shape[1]
    nf = d_ff // tf
    halo_blocks = tm // FFN_HALO
    return pl.pallas_call(
        functools.partial(_ffn_kernel, tm=tm, tiles_per_seq=seq // tm),
        grid=(m // tm, nf),
        in_specs=[
            pl.BlockSpec((tm, d), lambda i, j: (i, 0)),
            pl.BlockSpec((FFN_HALO, d), lambda i, j: (jnp.maximum(i * halo_blocks - 1, 0), 0)),
            pl.BlockSpec((1, d), lambda i, j: (0, 0)),
            pl.BlockSpec((None, d, tf), lambda i, j: (layer, 0, j)),
            pl.BlockSpec((None, d, tf), lambda i, j: (layer, 0, j + nf)),
            pl.BlockSpec((None, CONV_W, tf), lambda i, j: (layer, 0, j)),
            pl.BlockSpec((None, CONV_W, tf), lambda i, j: (layer, 0, j + nf)),
            pl.BlockSpec((None, 1, tf), lambda i, j: (layer, 0, j)),
            pl.BlockSpec((None, 1, tf), lambda i, j: (layer, 0, j + nf)),
            pl.BlockSpec((None, tf, d), lambda i, j: (layer, j, 0)),
            pl.BlockSpec((1, d), lambda i, j: (0, 0)),
        ],
        out_specs=pl.BlockSpec((tm, d), lambda i, j: (i, 0)),
        out_shape=jax.ShapeDtypeStruct((m, d), F32),
        scratch_shapes=[
            pltpu.VMEM((tm + FFN_HALO, d), BF16),
            pltpu.VMEM((tm + FFN_HALO, tf), F32),
            pltpu.VMEM((tm + FFN_HALO, tf), F32),
            pltpu.VMEM((tm, d), F32),
        ],
        compiler_params=pltpu.CompilerParams(
            dimension_semantics=("parallel", "arbitrary"), vmem_limit_bytes=VMEM_LIMIT),
        name="conv_ffn",
    )(h, h, g_pre, w_up, w_up, conv_w, conv_w, conv_b, conv_b, w_down, g_post)


def _outproj_kernel(ya_ref, yb_ref, w_ref, h_ref, g_ref, o_ref):
    ka = ya_ref.shape[1]
    mix = _dot(ya_ref[...], w_ref[:ka, :]) + _dot(yb_ref[...], w_ref[ka:, :])
    o_ref[...] = h_ref[...] + _rms(mix, g_ref[...])


def _outproj(ya, yb, w, h, g, *, tm=512):
    m, d = h.shape
    return pl.pallas_call(
        _outproj_kernel,
        grid=(m // tm,),
        in_specs=[
            pl.BlockSpec((tm, ya.shape[1]), lambda i: (i, 0)),
            pl.BlockSpec((tm, yb.shape[1]), lambda i: (i, 0)),
            pl.BlockSpec(w.shape, lambda i: (0, 0)),
            pl.BlockSpec((tm, d), lambda i: (i, 0)),
            pl.BlockSpec((1, d), lambda i: (0, 0)),
        ],
        out_specs=pl.BlockSpec((tm, d), lambda i: (i, 0)),
        out_shape=jax.ShapeDtypeStruct((m, d), F32),
        compiler_params=pltpu.CompilerParams(
            dimension_semantics=("parallel",), vmem_limit_bytes=VMEM_LIMIT),
        name="outproj",
    )(ya, yb, w, h, g)


def _sgu_kernel(u_ref, v_ref, lng_ref, lnb_ref, ws_ref, bs_ref, wo_ref, h_ref, g_ref, o_ref,
                vln_ref, gated_ref, *, tm):
    v = v_ref[...]
    mu = jnp.mean(v, axis=-1, keepdims=True)
    dv = v - mu
    var = jnp.mean(dv * dv, axis=-1, keepdims=True)
    vln_ref[...] = (dv * lax.rsqrt(var + 1e-5) * lng_ref[...] + lnb_ref[...]).astype(BF16)

    gw = v.shape[1] // SG_GROUPS
    ii = lax.broadcasted_iota(jnp.int32, (SG_CHUNK, SG_CHUNK), 0)
    jj = lax.broadcasted_iota(jnp.int32, (SG_CHUNK, SG_CHUNK), 1)
    visible = (jj // CHUNK) <= (ii // CHUNK)
    for g in range(SG_GROUPS):
        wm = jnp.where(visible, ws_ref[g], 0.0).astype(BF16)
        bcol = bs_ref[:, g:g + 1]
        cols = slice(g * gw, (g + 1) * gw)
        for c in range(tm // SG_CHUNK):
            rows = slice(c * SG_CHUNK, (c + 1) * SG_CHUNK)
            s = _dot(wm, vln_ref[rows, cols]) + bcol
            gated_ref[rows, cols] = (u_ref[rows, cols] * s).astype(BF16)

    mix = _dot(gated_ref[...], wo_ref[...])
    o_ref[...] = h_ref[...] + _rms(mix, g_ref[...])


def _sgu(z, ln_g, ln_b, w_s, b_s_t, w_out, h, g, *, tm=256):
    m, d = h.shape
    return pl.pallas_call(
        functools.partial(_sgu_kernel, tm=tm),
        grid=(m // tm,),
        in_specs=[
            pl.BlockSpec((tm, d), lambda i: (i, 0)),
            pl.BlockSpec((tm, d), lambda i: (i, 1)),
            pl.BlockSpec((1, d), lambda i: (0, 0)),
            pl.BlockSpec((1, d), lambda i: (0, 0)),
            pl.BlockSpec(w_s.shape, lambda i: (0, 0, 0)),
            pl.BlockSpec(b_s_t.shape, lambda i: (0, 0)),
            pl.BlockSpec(w_out.shape, lambda i: (0, 0)),
            pl.BlockSpec((tm, d), lambda i: (i, 0)),
            pl.BlockSpec((1, d), lambda i: (0, 0)),
        ],
        out_specs=pl.BlockSpec((tm, d), lambda i: (i, 0)),
        out_shape=jax.ShapeDtypeStruct((m, d), F32),
        scratch_shapes=[pltpu.VMEM((tm, d), BF16), pltpu.VMEM((tm, d), BF16)],
        compiler_params=pltpu.CompilerParams(
            dimension_semantics=("parallel",), vmem_limit_bytes=VMEM_LIMIT),
        name="sgu",
    )(z, z, ln_g, ln_b, w_s, b_s_t, w_out, h, g)


RW_PP_ROWS = 16
(PP_MU_R, PP_MU_K, PP_MU_V, PP_W0, PP_A0, PP_KK, PP_KA, PP_RK, PP_GNG, PP_GNB) = range(10)
RW_CARRY = 8
RW_GROUP = 8


def _rwkv_kernel(r_ref, k_ref, v_ref, xx_ref, xg_ref, pp_ref, mux_ref, mug_ref, wup_ref, aup_ref,
                 gup_ref, y_ref,
                 rb_ref, kb_ref, vb_ref, xb_ref, gb_ref,
                 r_s, k_s, v_s, lw_s, kk_s, be_s, g_s, bonus_s, y_s, ra_s, z_s, phi_s, psi_s, state_ref,
                 *, tt):
    L = RW_CHUNK
    tb = pl.program_id(2)
    pp = pp_ref[...]
    row = lambda i: pp[i:i + 1, :]

    lane = lax.broadcasted_iota(jnp.int32, (1, LANES), 1)
    head_masks = (lane < A_HEAD_DIM, lane >= A_HEAD_DIM)
    li = lax.broadcasted_iota(jnp.int32, (LANES, LANES), 0)
    lj = lax.broadcasted_iota(jnp.int32, (LANES, LANES), 1)
    same_head = (li // A_HEAD_DIM) == (lj // A_HEAD_DIM)

    def seg_sum(x):
        s0 = jnp.sum(jnp.where(head_masks[0], x, 0.0), axis=-1, keepdims=True)
        s1 = jnp.sum(jnp.where(head_masks[1], x, 0.0), axis=-1, keepdims=True)
        return jnp.where(head_masks[0], s0, s1)

    @pl.when(tb == 0)
    def _():
        for b in (rb_ref, kb_ref, vb_ref, xb_ref, gb_ref):
            b[0:RW_CARRY, :] = jnp.zeros((RW_CARRY, b.shape[1]), F32)
        state_ref[...] = jnp.zeros_like(state_ref)

    def shift_mix(x_ref, buf_ref, mu):
        x = x_ref[...]
        buf_ref[RW_CARRY:, :] = x
        prev = buf_ref[pl.ds(RW_CARRY - 1, tt), :]
        return x + (prev - x) * mu

    r = shift_mix(r_ref, rb_ref, row(PP_MU_R))
    k = shift_mix(k_ref, kb_ref, row(PP_MU_K))
    v = shift_mix(v_ref, vb_ref, row(PP_MU_V))
    xx = shift_mix(xx_ref, xb_ref, mux_ref[...])
    xg = shift_mix(xg_ref, gb_ref, mug_ref[...])
    for src, buf in ((r_ref, rb_ref), (k_ref, kb_ref), (v_ref, vb_ref), (xx_ref, xb_ref),
                     (xg_ref, gb_ref)):
        buf[0:RW_CARRY, :] = src[tt - RW_CARRY:tt, :]

    w_pre = row(PP_W0) + _dot(jnp.tanh(xx).astype(BF16), wup_ref[...])
    log_w = -(jnp.maximum(-w_pre, 0.0) + jnp.log1p(jnp.exp(-jnp.abs(w_pre)))) - 0.5
    lw_s[...] = -jnp.exp(log_w)
    a = jax.nn.sigmoid(row(PP_A0) + _dot(xx.astype(BF16), aup_ref[...]))
    g_s[...] = _dot(jax.nn.sigmoid(xg).astype(BF16), gup_ref[...])
    kk = k * row(PP_KK)
    kk = kk * lax.rsqrt(seg_sum(kk * kk) + 1e-12)
    k = k * (1.0 + (a - 1.0) * row(PP_KA))
    bonus_s[...] = seg_sum(r * k * row(PP_RK)) * v
    r_s[...] = r
    k_s[...] = k
    v_s[...] = v
    kk_s[...] = kk
    be_s[...] = kk * a

    tl = li % L
    cl = lj % L
    strict_bd = same_head & (cl < tl)
    incl_bd = same_head & (cl <= tl)
    tri_bd = jnp.where(incl_bd, 1.0, 0.0).astype(BF16)
    eye = jnp.where(li == lj, 1.0, 0.0).astype(F32)

    def stack(x):
        return jnp.concatenate([jnp.where(head_masks[0], x, 0.0), jnp.where(head_masks[1], x, 0.0)],
                               axis=0)

    state = [state_ref[...]]
    pending = []

    def tick():
        if pending:
            pending.pop(0)()

    def make_step(c):
        def step():
            s_hi, s_lo = _split2(state[0])
            y_st = _dot_nt(ra_s[c].astype(BF16), s_hi) + z_s[c]
            y_s[c * L:(c + 1) * L, :] = y_st[:L, :] + y_st[L:, :]
            p_hi, p_lo = _split2(phi_s[c])
            state[0] = _dot(s_hi, p_hi) + _dot(s_lo, p_hi) + _dot(s_hi, p_lo) + psi_s[c]
        return step

    def chunk_group(gi):
        G = range(RW_GROUP)
        cs, lw = [], []
        for half in range(RW_GROUP // 2):
            c0 = gi * RW_GROUP + 2 * half
            lw2 = lw_s[c0 * L:(c0 + 2) * L, :]
            h3 = _split3(lw2)
            cs2 = _dot(tri_bd, h3[0]) + _dot(tri_bd, h3[1]) + _dot(tri_bd, h3[2])
            cs += [cs2[:L, :], cs2[L:, :]]
            lw += [lw2[:L, :], lw2[L:, :]]
        tick()
        cidx = [gi * RW_GROUP + i for i in G]
        sls = [slice(c * L, (c + 1) * L) for c in cidx]
        e_pos = [jnp.exp(cs[i]) for i in G]
        e_prev = [jnp.exp(cs[i] - lw[i]) for i in G]
        e_neg = [jnp.exp(-cs[i]) for i in G]
        lam_l = [e_pos[i][L - 1:L, :] for i in G]
        rb_f = [stack(r_s[sls[i], :] * e_pos[i]) for i in G]
        ab = [stack(-kk_s[sls[i], :] * e_prev[i]).astype(BF16) for i in G]
        rb = [rb_f[i].astype(BF16) for i in G]
        bt = [stack(be_s[sls[i], :] * e_neg[i]).astype(BF16) for i in G]
        kt = [stack(k_s[sls[i], :] * e_neg[i]).astype(BF16) for i in G]
        v16 = [stack(v_s[sls[i], :]).astype(BF16) for i in G]
        tick()
        g = [_dot_nt(jnp.concatenate([ab[i], rb[i]], axis=0), jnp.concatenate([bt[i], kt[i]], axis=0))
             for i in G]
        a_ab = [jnp.where(strict_bd, g[i][:LANES, :LANES], 0.0) for i in G]
        a_ak = [jnp.where(strict_bd, g[i][:LANES, LANES:], 0.0).astype(BF16) for i in G]
        a_rb = [jnp.where(incl_bd, g[i][LANES:, :LANES], 0.0).astype(BF16) for i in G]
        a_rk = [jnp.where(incl_bd, g[i][LANES:, LANES:], 0.0).astype(BF16) for i in G]
        tick()
        t = [eye + a_ab[i] for i in G]
        p = a_ab
        for _ in range(5):
            p16 = [p[i].astype(BF16) for i in G]
            p = [_dot(p16[i], p16[i]) for i in G]
            tick()
            t = [t[i] + _dot(t[i].astype(BF16), p[i].astype(BF16)) for i in G]
            tick()
        t16 = [t[i].astype(BF16) for i in G]
        av = [_dot(a_ak[i], v16[i]).astype(BF16) for i in G]
        ta = [_dot(t16[i], ab[i]).astype(BF16) for i in G]
        tick()
        w16 = [_dot(t16[i], av[i]).astype(BF16) for i in G]
        for i in G:
            ra_s[cidx[i]] = rb_f[i] + _dot(a_rb[i], ta[i])
        tick()
        for i in G:
            z_s[cidx[i]] = _dot(a_rb[i], w16[i]) + _dot(a_rk[i], v16[i])
        tick()
        for i in G:
            phi_s[cidx[i]] = (eye + _dot_tn(ta[i], bt[i])) * lam_l[i]
        tick()
        for i in G:
            psi_s[cidx[i]] = (_dot_tn(w16[i], bt[i]) + _dot_tn(v16[i], kt[i])) * lam_l[i]
        while pending:
            tick()
        pending.extend(make_step(c) for c in cidx)

    for gi in range(tt // (L * RW_GROUP)):
        chunk_group(gi)
    while pending:
        tick()
    state_ref[...] = state[0]

    y = y_s[...]
    inv_n = 1.0 / A_HEAD_DIM
    mean = seg_sum(y) * inv_n
    d = y - mean
    var = seg_sum(d * d) * inv_n
    yn = d * lax.rsqrt(var + A_GN_EPS) * row(PP_GNG) + row(PP_GNB)
    y_ref[...] = ((yn + bonus_s[...]) * g_s[...]).astype(y_ref.dtype)


def _rwkv(z3, pp, mu_xx, mu_xg, wup, aup, gup, *, tt=1024):
    bsz, seq, _ = z3.shape
    pairs = A_WIDTH // LANES
    cb = lambda col, w: col // w
    col_spec = lambda col0: pl.BlockSpec((None, tt, LANES), lambda b, p, t: (b, t, cb(col0, LANES) + p))
    buf = lambda w: pltpu.VMEM((tt + RW_CARRY, w), F32)
    big = pltpu.VMEM((tt, LANES), F32)
    return pl.pallas_call(
        functools.partial(_rwkv_kernel, tt=tt),
        grid=(bsz, pairs, seq // tt),
        in_specs=[
            col_spec(COL_R), col_spec(COL_K), col_spec(COL_V),
            pl.BlockSpec((None, tt, LANES), lambda b, p, t: (b, t, cb(COL_XWXA, LANES))),
            pl.BlockSpec((None, tt, 256), lambda b, p, t: (b, t, cb(COL_XG, 256))),
            pl.BlockSpec((RW_PP_ROWS, LANES), lambda b, p, t: (0, p)),
            pl.BlockSpec((1, LANES), lambda b, p, t: (0, 0)),
            pl.BlockSpec((1, 256), lambda b, p, t: (0, 0)),
            pl.BlockSpec((LANES, LANES), lambda b, p, t: (0, p)),
            pl.BlockSpec((LANES, LANES), lambda b, p, t: (0, p)),
            pl.BlockSpec((256, LANES), lambda b, p, t: (0, p)),
        ],
        out_specs=pl.BlockSpec((None, tt, LANES), lambda b, p, t: (b, t, p)),
        out_shape=jax.ShapeDtypeStruct((bsz, seq, A_WIDTH), BF16),
        scratch_shapes=[buf(LANES), buf(LANES), buf(LANES), buf(LANES), buf(256)]
        + [big] * 9 + [pltpu.VMEM((tt // RW_CHUNK, LANES, LANES), F32)] * 4
        + [pltpu.VMEM((LANES, LANES), F32)],
        compiler_params=pltpu.CompilerParams(
            dimension_semantics=("parallel", "parallel", "arbitrary"), vmem_limit_bytes=VMEM_LIMIT),
        name="rwkv7",
    )(z3, z3, z3, z3, z3, pp, mu_xx, mu_xg, wup, aup, gup)


DSA_TK = 512
DSA_QB = 256
DSA_HG = 1
DSA_BATCH = 8
LOG2E = 1.4426950408889634


def _dsa_kernel(q_ref, qi_ref, kwq_ref, ckv_ref, kwf_ref, kvn_ref, ikn_ref, wuk_ref, wuvt_ref,
                o_ref, ckv_s, ckvt_s, ke_s, ko_s, key_s, key_hi_s, key_lo_s, cut_s, qlat_s, m_s, l_s,
                acc_s, *, seq):
    j = pl.program_id(1)
    tk = DSA_TK
    qb = DSA_QB
    lane = lax.broadcasted_iota(jnp.int32, (1, LANES), 1)
    qlane = lax.broadcasted_iota(jnp.int32, (1, qb), 1)

    @pl.when(j == 0)
    def _():
        def prep(t, carry):
            sl = pl.ds(pl.multiple_of(t * tk, tk), tk)
            ckv = _rms(ckv_ref[sl, :], kvn_ref[...])
            ckv_s[sl, :] = ckv.astype(BF16)
            ckvt_s[t] = ckv.T.astype(BF16)
            kw = kwf_ref[sl, :]
            ksq = jnp.where(lane < IDX_HEAD_DIM, kw * kw, 0.0)
            ms = jnp.sum(ksq, axis=-1, keepdims=True) * (1.0 / IDX_HEAD_DIM)
            kn = kw * lax.rsqrt(ms + NORM_EPS) * ikn_ref[...]
            ke_s[sl, :] = kn.astype(BF16)
            ko_s[sl, :] = pltpu.roll(kn, IDX_HEAD_DIM, axis=1).astype(BF16)
            return carry
        lax.fori_loop(0, seq // tk, prep, 0)

    n_tiles = (j * qb + qb + tk - 1) // tk
    limit = j * qb + (qlane // CHUNK + 1) * CHUNK
    krow = lax.broadcasted_iota(jnp.int32, (tk, 1), 0)

    w_t = kwq_ref[...].T * ((IDX_HEADS ** -0.5) * (IDX_HEAD_DIM ** -0.5))
    w_rows = [w_t[IDX_HEAD_DIM + h:IDX_HEAD_DIM + h + 1, :] for h in range(IDX_HEADS)]
    qi16 = [qi_ref[:, p * LANES:(p + 1) * LANES].astype(BF16) for p in range(IDX_HEADS // 2)]

    def score_tile(t, carry):
        sl = pl.ds(pl.multiple_of(t * tk, tk), tk)
        ke = ke_s[sl, :]
        ko = ko_s[sl, :]
        sc = jnp.zeros((tk, qb), F32)
        for p in range(IDX_HEADS // 2):
            xe = jnp.maximum(_dot_nt(ke, qi16[p]), 0.0)
            xo = jnp.maximum(_dot_nt(ko, qi16[p]), 0.0)
            sc = sc + (xe * w_rows[2 * p] + xo * w_rows[2 * p + 1])
        sc = jnp.where(t * tk + krow < limit, sc, -jnp.inf)
        bits = pltpu.bitcast(sc, jnp.int32)
        key = bits ^ ((bits >> 31) & 0x7FFFFFFF)
        key_s[t] = key
        key_hi_s[t] = (key >> 16).astype(jnp.int16)
        key_lo_s[t] = ((key & 0xFFFF) - 0x8000).astype(jnp.int16)
        return carry

    lax.fori_loop(0, n_tiles, score_tile, 0)

    rows16 = BF16_SUBLANES
    n_acc = 4

    def half16(x):
        return jnp.broadcast_to(x, (rows16, qb)).astype(jnp.int16)

    def count_ge16(src_s, c16):
        one = jnp.ones((), BF16)
        zero = jnp.zeros((), BF16)

        def body(t, accs):
            accs = list(accs)
            for i in range(tk // rows16):
                hit = jnp.where(src_s[t, i * rows16:(i + 1) * rows16, :] >= c16, one, zero)
                accs[i % n_acc] = accs[i % n_acc] + hit
            return tuple(accs)

        accs = lax.fori_loop(0, n_tiles, body, tuple(jnp.zeros((rows16, qb), BF16) for _ in range(n_acc)))
        total = (accs[0] + accs[1]) + (accs[2] + accs[3])
        return jnp.sum(total.astype(F32), axis=0, keepdims=True)

    def count_ge_hi(cand):
        return count_ge16(key_hi_s, half16(cand >> 16))

    def count_ge_lo(cand):
        return count_ge16(key_lo_s, half16((cand & 0xFFFF) - 0x8000))

    def count_where(pred):
        def body(t, part):
            hit = jnp.where(pred(key_s[t], t), 1.0, 0.0)
            return part + jnp.sum(hit.reshape(4, tk // 32, 8, qb), axis=1)
        part = lax.fori_loop(0, n_tiles, body, jnp.zeros((4, 8, qb), F32))
        return jnp.sum(jnp.sum(part, axis=0), axis=0, keepdims=True)

    def count_ge(cand):
        return count_where(lambda k, t: k >= cand)

    int_min = jnp.int32(-2 ** 31)
    topk = float(TOPK_MAX)
    theta = jnp.where(count_ge_hi(jnp.zeros((1, qb), jnp.int32)) >= topk, 0, int_min)

    def hi_step(i, theta):
        cand = theta | (jnp.int32(1) << (30 - i))
        return jnp.where(count_ge_hi(cand) >= topk, cand, theta)

    def lo_step(i, theta):
        cand = theta | (jnp.int32(1) << (15 - i))
        return jnp.where(count_ge_lo(cand) >= topk, cand, theta)

    theta = lax.fori_loop(0, 15, hi_step, theta)

    th_hi = half16(theta >> 16)

    def merge_halves(t, carry):
        for i in range(tk // rows16):
            rows = slice(i * rows16, (i + 1) * rows16)
            hi = key_hi_s[t, rows, :]
            key_lo_s[t, rows, :] = jnp.where(
                hi > th_hi, jnp.int16(0x7FFF),
                jnp.where(hi == th_hi, key_lo_s[t, rows, :], jnp.int16(-0x8000)))
        return carry

    lax.fori_loop(0, n_tiles, merge_halves, 0)
    theta = lax.fori_loop(0, 16, lo_step, theta)

    seq_bits = (seq - 1).bit_length()
    cut_s[...] = jnp.full_like(cut_s, seq)
    n_ge = count_ge(theta)

    @pl.when(jnp.max(n_ge) > topk)
    def _():
        n_gt = n_ge - count_where(lambda k, t: k == theta)
        keep = topk - n_gt

        def idx_step(i, cut):
            cand = cut | (jnp.int32(1) << (seq_bits - 1 - i))
            below = count_where(lambda k, t: (k == theta) & (t * tk + krow < cand))
            return jnp.where(below <= keep - 1.0, cand, cut)

        cut_s[...] = lax.fori_loop(0, seq_bits, idx_step, jnp.zeros((1, qb), jnp.int32))

    cut = cut_s[...]

    for h in range(B_HEADS):
        qh = q_ref[:, h * B_HEAD_DIM:(h + 1) * B_HEAD_DIM].astype(BF16)
        qlat_s[h * qb:(h + 1) * qb, :] = (
            _dot(qh, wuk_ref[h]) * ((B_HEAD_DIM ** -0.5) * LOG2E)).astype(BF16)

    m_s[...] = jnp.full_like(m_s, NEG_BIG)
    l_s[...] = jnp.zeros_like(l_s)
    acc_s[...] = jnp.zeros_like(acc_s)
    gw = DSA_HG * qb
    n_groups = B_HEADS // DSA_HG

    def attend(t, carry):
        ckv = ckv_s[pl.ds(pl.multiple_of(t * tk, tk), tk), :]
        ckvt = ckvt_s[t]
        keys = key_s[t]
        kpos = t * tk + krow
        sel = ((keys > theta) | ((keys == theta) & (kpos <= cut))) & (kpos < limit)
        bias1 = jnp.where(sel, 0.0, NEG_BIG)
        bias = bias1 if DSA_HG == 1 else jnp.concatenate([bias1] * DSA_HG, axis=1)
        for g0 in range(0, n_groups, DSA_BATCH):
            G = range(g0, min(g0 + DSA_BATCH, n_groups))
            cols = {g: slice(g * gw, (g + 1) * gw) for g in G}
            lg = {g: _dot_nt(ckv, qlat_s[cols[g], :]) + bias for g in G}
            m_old = {g: m_s[:, cols[g]] for g in G}
            m_new = {g: jnp.maximum(m_old[g], jnp.max(lg[g], axis=0, keepdims=True)) for g in G}
            alpha = {g: jnp.exp2(m_old[g] - m_new[g]) for g in G}
            psum, p16 = {}, {}
            for g in G:
                p = jnp.exp2(lg[g] - m_new[g])
                psum[g] = jnp.sum(p, axis=0, keepdims=True)
                p16[g] = p.astype(BF16)
            pv = {g: _dot(ckvt, p16[g]) for g in G}
            for g in G:
                l_s[:, cols[g]] = alpha[g] * l_s[:, cols[g]] + psum[g]
                acc_s[:, cols[g]] = alpha[g] * acc_s[:, cols[g]] + pv[g]
                m_s[:, cols[g]] = m_new[g]
        return carry

    lax.fori_loop(0, n_tiles, attend, 0)

    for h in range(B_HEADS):
        cols = slice(h * qb, (h + 1) * qb)
        o_t = (acc_s[:, cols] * (1.0 / l_s[:, cols])).astype(BF16)
        o_ref[:, h * B_HEAD_DIM:(h + 1) * B_HEAD_DIM] = _dot(wuvt_ref[h], o_t).T.astype(o_ref.dtype)


def _dsa(z3, kv_norm, ikn_pad, w_uk, w_uv):
    bsz, seq, _ = z3.shape
    w_uv = jnp.swapaxes(w_uv, 1, 2)
    qb = DSA_QB
    assert seq // BF16_SUBLANES <= 256, "bf16 partial counts of the top-k search must stay exact"
    return pl.pallas_call(
        functools.partial(_dsa_kernel, seq=seq),
        grid=(bsz, seq // qb),
        in_specs=[
            pl.BlockSpec((None, qb, B_WIDTH), lambda b, j: (b, j, COL_Q // B_WIDTH)),
            pl.BlockSpec((None, qb, B_WIDTH), lambda b, j: (b, j, COL_QI // B_WIDTH)),
            pl.BlockSpec((None, qb, LANES), lambda b, j: (b, j, COL_KW // LANES)),
            pl.BlockSpec((None, seq, KV_LATENT), lambda b, j: (b, 0, COL_CKV // KV_LATENT)),
            pl.BlockSpec((None, seq, LANES), lambda b, j: (b, 0, COL_KW // LANES)),
            pl.BlockSpec((1, KV_LATENT), lambda b, j: (0, 0)),
            pl.BlockSpec((1, LANES), lambda b, j: (0, 0)),
            pl.BlockSpec(w_uk.shape, lambda b, j: (0, 0, 0)),
            pl.BlockSpec(w_uv.shape, lambda b, j: (0, 0, 0)),
        ],
        out_specs=pl.BlockSpec((None, qb, B_WIDTH), lambda b, j: (b, j, 0)),
        out_shape=jax.ShapeDtypeStruct((bsz, seq, B_WIDTH), BF16),
        scratch_shapes=[
            pltpu.VMEM((seq, KV_LATENT), BF16),
            pltpu.VMEM((seq // DSA_TK, KV_LATENT, DSA_TK), BF16),
            pltpu.VMEM((seq, LANES), BF16),
            pltpu.VMEM((seq, LANES), BF16),
            pltpu.VMEM((seq // DSA_TK, DSA_TK, qb), jnp.int32),
            pltpu.VMEM((seq // DSA_TK, DSA_TK, qb), jnp.int16),
            pltpu.VMEM((seq // DSA_TK, DSA_TK, qb), jnp.int16),
            pltpu.VMEM((1, qb), jnp.int32),
            pltpu.VMEM((B_HEADS * qb, KV_LATENT), BF16),
            pltpu.VMEM((1, B_HEADS * qb), F32),
            pltpu.VMEM((1, B_HEADS * qb), F32),
            pltpu.VMEM((KV_LATENT, B_HEADS * qb), F32),
        ],
        compiler_params=pltpu.CompilerParams(
            dimension_semantics=("parallel", "arbitrary"), vmem_limit_bytes=VMEM_LIMIT),
        name="dsa",
    )(z3, z3, z3, z3, z3, kv_norm, ikn_pad, w_uk, w_uv)


def _pad_cols(w, width):
    return jnp.pad(w, ((0, 0), (0, width - w.shape[1])))


def _layer0_params(ab_w_in, mu, w0, w_up, a0, a_up, g_up, k_k, k_a, r_k, gn_g, gn_b, idx_k_norm):
    a_in = 3 * A_WIDTH + LORA_W + LORA_A + LORA_G
    o = 3 * A_WIDTH
    ab_w_in = ab_w_in.astype(BF16)
    w_r, w_k, w_v = ab_w_in[:, :A_WIDTH], ab_w_in[:, A_WIDTH:2 * A_WIDTH], ab_w_in[:, 2 * A_WIDTH:o]
    w_xwxa = ab_w_in[:, o:o + LORA_W + LORA_A]
    w_xg = ab_w_in[:, o + LORA_W + LORA_A:a_in]
    wb = ab_w_in[:, a_in:]
    o1 = B_WIDTH
    o2 = o1 + KV_LATENT
    o3 = o2 + IDX_HEADS * IDX_HEAD_DIM
    w_q, w_ckv, w_qi, w_kw = wb[:, :o1], wb[:, o1:o2], wb[:, o2:o3], wb[:, o3:]
    w_in = jnp.concatenate(
        [w_r, w_k, w_v, w_q, w_qi, w_ckv, _pad_cols(w_xg, 256), w_xwxa, _pad_cols(w_kw, LANES)], axis=1)
    w_in = _pad_cols(w_in, AB_IN_PAD)

    rows = [mu[:A_WIDTH], mu[A_WIDTH:2 * A_WIDTH], mu[2 * A_WIDTH:o], w0, a0, k_k, k_a,
            r_k.reshape(-1), gn_g, gn_b]
    pp = jnp.stack(rows + [jnp.zeros_like(w0)] * (RW_PP_ROWS - len(rows)), axis=0)
    mu_xx = mu[o:o + LORA_W + LORA_A][None, :]
    mu_xg = jnp.pad(mu[o + LORA_W + LORA_A:a_in], (0, 256 - LORA_G))[None, :]
    wup = jnp.pad(w_up, ((0, LANES - LORA_W), (0, 0))).astype(BF16)
    aup = jnp.pad(a_up, ((LORA_W, 0), (0, 0))).astype(BF16)
    gup = jnp.pad(g_up, ((0, 256 - LORA_G), (0, 0))).astype(BF16)
    ikn = jnp.pad(idx_k_norm, (0, LANES - IDX_HEAD_DIM))[None, :]
    return w_in, pp, mu_xx, mu_xg, wup, aup, gup, ikn


def kernel(x, norm_g, ab_w_in, ab_shift_mu, rw_w0, rw_w_up, rw_a0, rw_a_up, rw_g_up, rw_k_k, rw_k_a, rw_r_k, rw_gn_g, rw_gn_b, mla_kv_norm, mla_w_uk, mla_w_uv, idx_k_norm, ab_w_out, sg_w_in, sg_b_in, sg_ln_g, sg_ln_b, sg_w_s, sg_b_s, sg_w_out, ffn_w_up, ffn_conv_w, ffn_conv_b, ffn_w_down):
    bsz, seq, d = x.shape
    n = bsz * seq
    depth = norm_g.shape[0]
    h = x.reshape(n, d)
    ffn_w_up16 = ffn_w_up.astype(BF16)
    ffn_w_down16 = ffn_w_down.astype(BF16)
    for layer in range(depth):
        g_pre_mix, g_post_mix, g_pre_ffn, g_post_ffn = [norm_g[layer, i][None, :] for i in range(4)]
        if layer % 2 == 0:
            e = layer // 2
            w_in, pp, mu_xx, mu_xg, wup, aup, gup, ikn = _layer0_params(
                ab_w_in[e], ab_shift_mu[e], rw_w0[e], rw_w_up[e], rw_a0[e], rw_a_up[e], rw_g_up[e],
                rw_k_k[e], rw_k_a[e], rw_r_k[e], rw_gn_g[e], rw_gn_b[e], idx_k_norm[e])
            z = _norm_matmul(h, g_pre_mix, w_in, jnp.zeros((1, AB_IN_PAD), F32), gelu=False)
            z3 = z.reshape(bsz, seq, AB_IN_PAD)
            y_a = _rwkv(z3, pp, mu_xx, mu_xg, wup, aup, gup)
            y_b = _dsa(z3, mla_kv_norm[e][None, :], ikn, mla_w_uk[e].astype(BF16),
                       mla_w_uv[e].astype(BF16))
            h = _outproj(y_a.reshape(n, A_WIDTH), y_b.reshape(n, B_WIDTH),
                         ab_w_out[e].astype(BF16), h, g_post_mix)
        else:
            o = layer // 2
            z = _norm_matmul(h, g_pre_mix, sg_w_in[o].astype(BF16), sg_b_in[o][None, :], gelu=True)
            h = _sgu(z, sg_ln_g[o][None, :], sg_ln_b[o][None, :], sg_w_s[o], sg_b_s[o].T,
                     sg_w_out[o].astype(BF16), h, g_post_mix)
        h = _ffn(h, g_pre_ffn, g_post_ffn, ffn_w_up16, ffn_conv_w, ffn_conv_b[:, None, :], ffn_w_down16,
                 layer, seq=seq)
    return h.reshape(bsz, seq, d)
```

```python
import functools

import jax
import jax.numpy as jnp
from jax import lax
from jax.experimental import pallas as pl
from jax.experimental.pallas import tpu as pltpu

F32 = jnp.float32
BF16 = jnp.bfloat16

NORM_EPS = 1e-6
CHUNK = 64
A_HEADS = 16
A_HEAD_DIM = 64
A_WIDTH = A_HEADS * A_HEAD_DIM
LORA_W = 64
LORA_A = 64
LORA_G = 160
A_GN_EPS = 64e-5
B_HEADS = 8
B_HEAD_DIM = 128
B_WIDTH = B_HEADS * B_HEAD_DIM
KV_LATENT = 256
IDX_HEADS = 16
IDX_HEAD_DIM = 64
TOPK_MAX = 256
SG_CHUNK = 128
SG_GROUPS = 8
CONV_W = 3

LANES = 128
BF16_SUBLANES = 16
VMEM_LIMIT = 56 * 1024 * 1024

COL_R = 0
COL_K = A_WIDTH
COL_V = 2 * A_WIDTH
COL_Q = 3 * A_WIDTH
COL_QI = COL_Q + B_WIDTH
COL_CKV = COL_QI + IDX_HEADS * IDX_HEAD_DIM
COL_XG = COL_CKV + KV_LATENT
COL_XWXA = COL_XG + 256
COL_KW = COL_XWXA + LANES
AB_IN_PAD = 6144

RW_CHUNK = 64
NEG_BIG = -1e30


def _rms(x, g):
    ms = jnp.mean(x * x, axis=-1, keepdims=True)
    return x * lax.rsqrt(ms + NORM_EPS) * g


def _dot(a, b):
    return jnp.dot(a, b, preferred_element_type=F32)


def _dot_nt(a, b):
    return lax.dot_general(a, b, (((1,), (1,)), ((), ())), preferred_element_type=F32)


def _dot_tn(a, b):
    return lax.dot_general(a, b, (((0,), (0,)), ((), ())), preferred_element_type=F32)


def _split2(x):
    hi = x.astype(BF16)
    lo = (x - hi.astype(F32)).astype(BF16)
    return hi, lo


def _split3(x):
    hi = x.astype(BF16)
    r1 = x - hi.astype(F32)
    mid = r1.astype(BF16)
    lo = (r1 - mid.astype(F32)).astype(BF16)
    return hi, mid, lo


def _norm_mm_kernel(x_ref, g_ref, w_ref, b_ref, o_ref, xn_ref, *, gelu):
    @pl.when(pl.program_id(1) == 0)
    def _():
        xn_ref[...] = _rms(x_ref[...], g_ref[...]).astype(BF16)

    y = _dot(xn_ref[...], w_ref[...])
    if gelu:
        y = y + b_ref[...]
        y = 0.5 * y * (1.0 + lax.erf(y * 0.7071067811865476))
    o_ref[...] = y.astype(o_ref.dtype)


def _norm_matmul(x, g, w, b, *, gelu, tm=1024, tn=1024, out_dtype=F32):
    m, k = x.shape
    n = w.shape[1]
    return pl.pallas_call(
        functools.partial(_norm_mm_kernel, gelu=gelu),
        grid=(m // tm, n // tn),
        in_specs=[
            pl.BlockSpec((tm, k), lambda i, j: (i, 0)),
            pl.BlockSpec((1, k), lambda i, j: (0, 0)),
            pl.BlockSpec((k, tn), lambda i, j: (0, j)),
            pl.BlockSpec((1, tn), lambda i, j: (0, j)),
        ],
        out_specs=pl.BlockSpec((tm, tn), lambda i, j: (i, j)),
        out_shape=jax.ShapeDtypeStruct((m, n), out_dtype),
        scratch_shapes=[pltpu.VMEM((tm, k), BF16)],
        compiler_params=pltpu.CompilerParams(
            dimension_semantics=("parallel", "arbitrary"), vmem_limit_bytes=VMEM_LIMIT),
        name="norm_matmul",
    )(x, g, w, b)


FFN_HALO = BF16_SUBLANES


def _ffn_kernel(x_ref, xp_ref, gpre_ref, wg_ref, wu_ref, cwg_ref, cwu_ref, cbg_ref, cbu_ref,
                wd_ref, gpost_ref, o_ref, xn_ref, hg_ref, hu_ref, acc_ref, *, tm, tiles_per_seq):
    i = pl.program_id(0)
    j = pl.program_id(1)

    @pl.when(j == 0)
    def _():
        xn_ref[FFN_HALO:, :] = _rms(x_ref[...], gpre_ref[...]).astype(BF16)
        keep = jnp.where(i % tiles_per_seq != 0, 1.0, 0.0)
        xn_ref[:FFN_HALO, :] = (_rms(xp_ref[...], gpre_ref[...]) * keep).astype(BF16)
        acc_ref[...] = jnp.zeros_like(acc_ref)

    xn = xn_ref[...]
    hg_ref[...] = _dot(xn, wg_ref[...])
    hu_ref[...] = _dot(xn, wu_ref[...])

    def conv(h_ref, cw_ref, cb_ref):
        cw = cw_ref[...]
        return (h_ref[pl.ds(FFN_HALO, tm), :] * cw[2:3, :]
                + h_ref[pl.ds(FFN_HALO - 1, tm), :] * cw[1:2, :]
                + h_ref[pl.ds(FFN_HALO - 2, tm), :] * cw[0:1, :]
                + cb_ref[...])

    gate = conv(hg_ref, cwg_ref, cbg_ref)
    up = conv(hu_ref, cwu_ref, cbu_ref)
    act = gate * jax.nn.sigmoid(gate) * up
    acc_ref[...] += _dot(act.astype(BF16), wd_ref[...])

    @pl.when(j == pl.num_programs(1) - 1)
    def _():
        o_ref[...] = x_ref[...] + _rms(acc_ref[...], gpost_ref[...])


def _ffn(h, g_pre, g_post, w_up, conv_w, conv_b, w_down, layer, *, seq, tm=512, tf=512):
    m, d = h.shape
    d_ff = w_down.shape[1]
    nf = d_ff // tf
    halo_blocks = tm // FFN_HALO
    return pl.pallas_call(
        functools.partial(_ffn_kernel, tm=tm, tiles_per_seq=seq // tm),
        grid=(m // tm, nf),
        in_specs=[
            pl.BlockSpec((tm, d), lambda i, j: (i, 0)),
            pl.BlockSpec((FFN_HALO, d), lambda i, j: (jnp.maximum(i * halo_blocks - 1, 0), 0)),
            pl.BlockSpec((1, d), lambda i, j: (0, 0)),
            pl.BlockSpec((None, d, tf), lambda i, j: (layer, 0, j)),
            pl.BlockSpec((None, d, tf), lambda i, j: (layer, 0, j + nf)),
            pl.BlockSpec((None, CONV_W, tf), lambda i, j: (layer, 0, j)),
            pl.BlockSpec((None, CONV_W, tf), lambda i, j: (layer, 0, j + nf)),
            pl.BlockSpec((None, 1, tf), lambda i, j: (layer, 0, j)),
            pl.BlockSpec((None, 1, tf), lambda i, j: (layer, 0, j + nf)),
            pl.BlockSpec((None, tf, d), lambda i, j: (layer, j, 0)),
            pl.BlockSpec((1, d), lambda i, j: (0, 0)),
        ],
        out_specs=pl.BlockSpec((tm, d), lambda i, j: (i, 0)),
        out_shape=jax.ShapeDtypeStruct((m, d), F32),
        scratch_shapes=[
            pltpu.VMEM((tm + FFN_HALO, d), BF16),
            pltpu.VMEM((tm + FFN_HALO, tf), F32),
            pltpu.VMEM((tm + FFN_HALO, tf), F32),
            pltpu.VMEM((tm, d), F32),
        ],
        compiler_params=pltpu.CompilerParams(
            dimension_semantics=("parallel", "arbitrary"), vmem_limit_bytes=VMEM_LIMIT),
        name="conv_ffn",
    )(h, h, g_pre, w_up, w_up, conv_w, conv_w, conv_b, conv_b, w_down, g_post)


def _outproj_kernel(ya_ref, yb_ref, w_ref, h_ref, g_ref, o_ref):
    ka = ya_ref.shape[1]
    mix = _dot(ya_ref[...], w_ref[:ka, :]) + _dot(yb_ref[...], w_ref[ka:, :])
    o_ref[...] = h_ref[...] + _rms(mix, g_ref[...])


def _outproj(ya, yb, w, h, g, *, tm=512):
    m, d = h.shape
    return pl.pallas_call(
        _outproj_kernel,
        grid=(m // tm,),
        in_specs=[
            pl.BlockSpec((tm, ya.shape[1]), lambda i: (i, 0)),
            pl.BlockSpec((tm, yb.shape[1]), lambda i: (i, 0)),
            pl.BlockSpec(w.shape, lambda i: (0, 0)),
            pl.BlockSpec((tm, d), lambda i: (i, 0)),
            pl.BlockSpec((1, d), lambda i: (0, 0)),
        ],
        out_specs=pl.BlockSpec((tm, d), lambda i: (i, 0)),
        out_shape=jax.ShapeDtypeStruct((m, d), F32),
        compiler_params=pltpu.CompilerParams(
            dimension_semantics=("parallel",), vmem_limit_bytes=VMEM_LIMIT),
        name="outproj",
    )(ya, yb, w, h, g)


def _sgu_kernel(u_ref, v_ref, lng_ref, lnb_ref, ws_ref, bs_ref, wo_ref, h_ref, g_ref, o_ref,
                vln_ref, gated_ref, *, tm):
    v = v_ref[...]
    mu = jnp.mean(v, axis=-1, keepdims=True)
    dv = v - mu
    var = jnp.mean(dv * dv, axis=-1, keepdims=True)
    vln_ref[...] = (dv * lax.rsqrt(var + 1e-5) * lng_ref[...] + lnb_ref[...]).astype(BF16)

    gw = v.shape[1] // SG_GROUPS
    ii = lax.broadcasted_iota(jnp.int32, (SG_CHUNK, SG_CHUNK), 0)
    jj = lax.broadcasted_iota(jnp.int32, (SG_CHUNK, SG_CHUNK), 1)
    visible = (jj // CHUNK) <= (ii // CHUNK)
    for g in range(SG_GROUPS):
        wm = jnp.where(visible, ws_ref[g], 0.0).astype(BF16)
        bcol = bs_ref[:, g:g + 1]
        cols = slice(g * gw, (g + 1) * gw)
        for c in range(tm // SG_CHUNK):
            rows = slice(c * SG_CHUNK, (c + 1) * SG_CHUNK)
            s = _dot(wm, vln_ref[rows, cols]) + bcol
            gated_ref[rows, cols] = (u_ref[rows, cols] * s).astype(BF16)

    mix = _dot(gated_ref[...], wo_ref[...])
    o_ref[...] = h_ref[...] + _rms(mix, g_ref[...])


def _sgu(z, ln_g, ln_b, w_s, b_s_t, w_out, h, g, *, tm=256):
    m, d = h.shape
    return pl.pallas_call(
        functools.partial(_sgu_kernel, tm=tm),
        grid=(m // tm,),
        in_specs=[
            pl.BlockSpec((tm, d), lambda i: (i, 0)),
            pl.BlockSpec((tm, d), lambda i: (i, 1)),
            pl.BlockSpec((1, d), lambda i: (0, 0)),
            pl.BlockSpec((1, d), lambda i: (0, 0)),
            pl.BlockSpec(w_s.shape, lambda i: (0, 0, 0)),
            pl.BlockSpec(b_s_t.shape, lambda i: (0, 0)),
            pl.BlockSpec(w_out.shape, lambda i: (0, 0)),
            pl.BlockSpec((tm, d), lambda i: (i, 0)),
            pl.BlockSpec((1, d), lambda i: (0, 0)),
        ],
        out_specs=pl.BlockSpec((tm, d), lambda i: (i, 0)),
        out_shape=jax.ShapeDtypeStruct((m, d), F32),
        scratch_shapes=[pltpu.VMEM((tm, d), BF16), pltpu.VMEM((tm, d), BF16)],
        compiler_params=pltpu.CompilerParams(
            dimension_semantics=("parallel",), vmem_limit_bytes=VMEM_LIMIT),
        name="sgu",
    )(z, z, ln_g, ln_b, w_s, b_s_t, w_out, h, g)


RW_PP_ROWS = 16
(PP_MU_R, PP_MU_K, PP_MU_V, PP_W0, PP_A0, PP_KK, PP_KA, PP_RK, PP_GNG, PP_GNB) = range(10)
RW_CARRY = 8
RW_GROUP = 8


def _rwkv_kernel(r_ref, k_ref, v_ref, xx_ref, xg_ref, pp_ref, mux_ref, mug_ref, wup_ref, aup_ref,
                 gup_ref, y_ref,
                 rb_ref, kb_ref, vb_ref, xb_ref, gb_ref,
                 r_s, k_s, v_s, lw_s, kk_s, be_s, g_s, bonus_s, y_s, ra_s, z_s, phi_s, psi_s, state_ref,
                 *, tt):
    L = RW_CHUNK
    tb = pl.program_id(2)
    pp = pp_ref[...]
    row = lambda i: pp[i:i + 1, :]

    lane = lax.broadcasted_iota(jnp.int32, (1, LANES), 1)
    head_masks = (lane < A_HEAD_DIM, lane >= A_HEAD_DIM)
    li = lax.broadcasted_iota(jnp.int32, (LANES, LANES), 0)
    lj = lax.broadcasted_iota(jnp.int32, (LANES, LANES), 1)
    same_head = (li // A_HEAD_DIM) == (lj // A_HEAD_DIM)

    def seg_sum(x):
        s0 = jnp.sum(jnp.where(head_masks[0], x, 0.0), axis=-1, keepdims=True)
        s1 = jnp.sum(jnp.where(head_masks[1], x, 0.0), axis=-1, keepdims=True)
        return jnp.where(head_masks[0], s0, s1)

    @pl.when(tb == 0)
    def _():
        for b in (rb_ref, kb_ref, vb_ref, xb_ref, gb_ref):
            b[0:RW_CARRY, :] = jnp.zeros((RW_CARRY, b.shape[1]), F32)
        state_ref[...] = jnp.zeros_like(state_ref)

    def shift_mix(x_ref, buf_ref, mu):
        x = x_ref[...]
        buf_ref[RW_CARRY:, :] = x
        prev = buf_ref[pl.ds(RW_CARRY - 1, tt), :]
        return x + (prev - x) * mu

    r = shift_mix(r_ref, rb_ref, row(PP_MU_R))
    k = shift_mix(k_ref, kb_ref, row(PP_MU_K))
    v = shift_mix(v_ref, vb_ref, row(PP_MU_V))
    xx = shift_mix(xx_ref, xb_ref, mux_ref[...])
    xg = shift_mix(xg_ref, gb_ref, mug_ref[...])
    for src, buf in ((r_ref, rb_ref), (k_ref, kb_ref), (v_ref, vb_ref), (xx_ref, xb_ref),
                     (xg_ref, gb_ref)):
        buf[0:RW_CARRY, :] = src[tt - RW_CARRY:tt, :]

    w_pre = row(PP_W0) + _dot(jnp.tanh(xx).astype(BF16), wup_ref[...])
    log_w = -(jnp.maximum(-w_pre, 0.0) + jnp.log1p(jnp.exp(-jnp.abs(w_pre)))) - 0.5
    lw_s[...] = -jnp.exp(log_w)
    a = jax.nn.sigmoid(row(PP_A0) + _dot(xx.astype(BF16), aup_ref[...]))
    g_s[...] = _dot(jax.nn.sigmoid(xg).astype(BF16), gup_ref[...])
    kk = k * row(PP_KK)
    kk = kk * lax.rsqrt(seg_sum(kk * kk) + 1e-12)
    k = k * (1.0 + (a - 1.0) * row(PP_KA))
    bonus_s[...] = seg_sum(r * k * row(PP_RK)) * v
    r_s[...] = r
    k_s[...] = k
    v_s[...] = v
    kk_s[...] = kk
    be_s[...] = kk * a

    tl = li % L
    cl = lj % L
    strict_bd = same_head & (cl < tl)
    incl_bd = same_head & (cl <= tl)
    tri_bd = jnp.where(incl_bd, 1.0, 0.0).astype(BF16)
    eye = jnp.where(li == lj, 1.0, 0.0).astype(F32)

    def stack(x):
        return jnp.concatenate([jnp.where(head_masks[0], x, 0.0), jnp.where(head_masks[1], x, 0.0)],
                               axis=0)

    state = [state_ref[...]]
    pending = []

    def tick():
        if pending:
            pending.pop(0)()

    def make_step(c):
        def step():
            s_hi, s_lo = _split2(state[0])
            y_st = _dot_nt(ra_s[c].astype(BF16), s_hi) + z_s[c]
            y_s[c * L:(c + 1) * L, :] = y_st[:L, :] + y_st[L:, :]
            p_hi, p_lo = _split2(phi_s[c])
            state[0] = _dot(s_hi, p_hi) + _dot(s_lo, p_hi) + _dot(s_hi, p_lo) + psi_s[c]
        return step

    def chunk_group(gi):
        G = range(RW_GROUP)
        cs, lw = [], []
        for half in range(RW_GROUP // 2):
            c0 = gi * RW_GROUP + 2 * half
            lw2 = lw_s[c0 * L:(c0 + 2) * L, :]
            h3 = _split3(lw2)
            cs2 = _dot(tri_bd, h3[0]) + _dot(tri_bd, h3[1]) + _dot(tri_bd, h3[2])
            cs += [cs2[:L, :], cs2[L:, :]]
            lw += [lw2[:L, :], lw2[L:, :]]
        tick()
        cidx = [gi * RW_GROUP + i for i in G]
        sls = [slice(c * L, (c + 1) * L) for c in cidx]
        e_pos = [jnp.exp(cs[i]) for i in G]
        e_prev = [jnp.exp(cs[i] - lw[i]) for i in G]
        e_neg = [jnp.exp(-cs[i]) for i in G]
        lam_l = [e_pos[i][L - 1:L, :] for i in G]
        rb_f = [stack(r_s[sls[i], :] * e_pos[i]) for i in G]
        ab = [stack(-kk_s[sls[i], :] * e_prev[i]).astype(BF16) for i in G]
        rb = [rb_f[i].astype(BF16) for i in G]
        bt = [stack(be_s[sls[i], :] * e_neg[i]).astype(BF16) for i in G]
        kt = [stack(k_s[sls[i], :] * e_neg[i]).astype(BF16) for i in G]
        v16 = [stack(v_s[sls[i], :]).astype(BF16) for i in G]
        tick()
        g = [_dot_nt(jnp.concatenate([ab[i], rb[i]], axis=0), jnp.concatenate([bt[i], kt[i]], axis=0))
             for i in G]
        a_ab = [jnp.where(strict_bd, g[i][:LANES, :LANES], 0.0) for i in G]
        a_ak = [jnp.where(strict_bd, g[i][:LANES, LANES:], 0.0).astype(BF16) for i in G]
        a_rb = [jnp.where(incl_bd, g[i][LANES:, :LANES], 0.0).astype(BF16) for i in G]
        a_rk = [jnp.where(incl_bd, g[i][LANES:, LANES:], 0.0).astype(BF16) for i in G]
        tick()
        t = [eye + a_ab[i] for i in G]
        p = a_ab
        for _ in range(5):
            p16 = [p[i].astype(BF16) for i in G]
            p = [_dot(p16[i], p16[i]) for i in G]
            tick()
            t = [t[i] + _dot(t[i].astype(BF16), p[i].astype(BF16)) for i in G]
            tick()
        t16 = [t[i].astype(BF16) for i in G]
        av = [_dot(a_ak[i], v16[i]).astype(BF16) for i in G]
        ta = [_dot(t16[i], ab[i]).astype(BF16) for i in G]
        tick()
        w16 = [_dot(t16[i], av[i]).astype(BF16) for i in G]
        for i in G:
            ra_s[cidx[i]] = rb_f[i] + _dot(a_rb[i], ta[i])
        tick()
        for i in G:
            z_s[cidx[i]] = _dot(a_rb[i], w16[i]) + _dot(a_rk[i], v16[i])
        tick()
        for i in G:
            phi_s[cidx[i]] = (eye + _dot_tn(ta[i], bt[i])) * lam_l[i]
        tick()
        for i in G:
            psi_s[cidx[i]] = (_dot_tn(w16[i], bt[i]) + _dot_tn(v16[i], kt[i])) * lam_l[i]
        while pending:
            tick()
        pending.extend(make_step(c) for c in cidx)

    for gi in range(tt // (L * RW_GROUP)):
        chunk_group(gi)
    while pending:
        tick()
    state_ref[...] = state[0]

    y = y_s[...]
    inv_n = 1.0 / A_HEAD_DIM
    mean = seg_sum(y) * inv_n
    d = y - mean
    var = seg_sum(d * d) * inv_n
    yn = d * lax.rsqrt(var + A_GN_EPS) * row(PP_GNG) + row(PP_GNB)
    y_ref[...] = ((yn + bonus_s[...]) * g_s[...]).astype(y_ref.dtype)


def _rwkv(z3, pp, mu_xx, mu_xg, wup, aup, gup, *, tt=1024):
    bsz, seq, _ = z3.shape
    pairs = A_WIDTH // LANES
    cb = lambda col, w: col // w
    col_spec = lambda col0: pl.BlockSpec((None, tt, LANES), lambda b, p, t: (b, t, cb(col0, LANES) + p))
    buf = lambda w: pltpu.VMEM((tt + RW_CARRY, w), F32)
    big = pltpu.VMEM((tt, LANES), F32)
    return pl.pallas_call(
        functools.partial(_rwkv_kernel, tt=tt),
        grid=(bsz, pairs, seq // tt),
        in_specs=[
            col_spec(COL_R), col_spec(COL_K), col_spec(COL_V),
            pl.BlockSpec((None, tt, LANES), lambda b, p, t: (b, t, cb(COL_XWXA, LANES))),
            pl.BlockSpec((None, tt, 256), lambda b, p, t: (b, t, cb(COL_XG, 256))),
            pl.BlockSpec((RW_PP_ROWS, LANES), lambda b, p, t: (0, p)),
            pl.BlockSpec((1, LANES), lambda b, p, t: (0, 0)),
            pl.BlockSpec((1, 256), lambda b, p, t: (0, 0)),
            pl.BlockSpec((LANES, LANES), lambda b, p, t: (0, p)),
            pl.BlockSpec((LANES, LANES), lambda b, p, t: (0, p)),
            pl.BlockSpec((256, LANES), lambda b, p, t: (0, p)),
        ],
        out_specs=pl.BlockSpec((None, tt, LANES), lambda b, p, t: (b, t, p)),
        out_shape=jax.ShapeDtypeStruct((bsz, seq, A_WIDTH), BF16),
        scratch_shapes=[buf(LANES), buf(LANES), buf(LANES), buf(LANES), buf(256)]
        + [big] * 9 + [pltpu.VMEM((tt // RW_CHUNK, LANES, LANES), F32)] * 4
        + [pltpu.VMEM((LANES, LANES), F32)],
        compiler_params=pltpu.CompilerParams(
            dimension_semantics=("parallel", "parallel", "arbitrary"), vmem_limit_bytes=VMEM_LIMIT),
        name="rwkv7",
    )(z3, z3, z3, z3, z3, pp, mu_xx, mu_xg, wup, aup, gup)


DSA_TK = 512
DSA_QB = 256
DSA_HG = 1
LOG2E = 1.4426950408889634


def _dsa_kernel(q_ref, qi_ref, kwq_ref, ckv_ref, kwf_ref, kvn_ref, ikn_ref, wuk_ref, wuvt_ref,
                o_ref, ckv_s, ckvt_s, ke_s, ko_s, key_s, key_hi_s, key_lo_s, cut_s, qlat_s, m_s, l_s,
                acc_s, *, seq):
    j = pl.program_id(1)
    tk = DSA_TK
    qb = DSA_QB
    lane = lax.broadcasted_iota(jnp.int32, (1, LANES), 1)
    qlane = lax.broadcasted_iota(jnp.int32, (1, qb), 1)

    @pl.when(j == 0)
    def _():
        def prep(t, carry):
            sl = pl.ds(pl.multiple_of(t * tk, tk), tk)
            ckv = _rms(ckv_ref[sl, :], kvn_ref[...])
            ckv_s[sl, :] = ckv.astype(BF16)
            ckvt_s[t] = ckv.T.astype(BF16)
            kw = kwf_ref[sl, :]
            ksq = jnp.where(lane < IDX_HEAD_DIM, kw * kw, 0.0)
            ms = jnp.sum(ksq, axis=-1, keepdims=True) * (1.0 / IDX_HEAD_DIM)
            kn = kw * lax.rsqrt(ms + NORM_EPS) * ikn_ref[...]
            ke_s[sl, :] = kn.astype(BF16)
            ko_s[sl, :] = pltpu.roll(kn, IDX_HEAD_DIM, axis=1).astype(BF16)
            return carry
        lax.fori_loop(0, seq // tk, prep, 0)

    n_tiles = (j * qb + qb + tk - 1) // tk
    limit = j * qb + (qlane // CHUNK + 1) * CHUNK
    krow = lax.broadcasted_iota(jnp.int32, (tk, 1), 0)

    w_t = kwq_ref[...].T * ((IDX_HEADS ** -0.5) * (IDX_HEAD_DIM ** -0.5))
    w_rows = [w_t[IDX_HEAD_DIM + h:IDX_HEAD_DIM + h + 1, :] for h in range(IDX_HEADS)]
    qi16 = [qi_ref[:, p * LANES:(p + 1) * LANES].astype(BF16) for p in range(IDX_HEADS // 2)]

    def score_tile(t, carry):
        sl = pl.ds(pl.multiple_of(t * tk, tk), tk)
        ke = ke_s[sl, :]
        ko = ko_s[sl, :]
        sc = jnp.zeros((tk, qb), F32)
        for p in range(IDX_HEADS // 2):
            xe = jnp.maximum(_dot_nt(ke, qi16[p]), 0.0)
            xo = jnp.maximum(_dot_nt(ko, qi16[p]), 0.0)
            sc = sc + (xe * w_rows[2 * p] + xo * w_rows[2 * p + 1])
        sc = jnp.where(t * tk + krow < limit, sc, -jnp.inf)
        bits = pltpu.bitcast(sc, jnp.int32)
        key = bits ^ ((bits >> 31) & 0x7FFFFFFF)
        key_s[t] = key
        key_hi_s[t] = (key >> 16).astype(jnp.int16)
        key_lo_s[t] = ((key & 0xFFFF) - 0x8000).astype(jnp.int16)
        return carry

    lax.fori_loop(0, n_tiles, score_tile, 0)

    rows16 = BF16_SUBLANES
    n_acc = 4

    def half16(x):
        return jnp.broadcast_to(x, (rows16, qb)).astype(jnp.int16)

    def count_ge16(src_s, c16):
        one = jnp.ones((), BF16)
        zero = jnp.zeros((), BF16)

        def body(t, accs):
            accs = list(accs)
            for i in range(tk // rows16):
                hit = jnp.where(src_s[t, i * rows16:(i + 1) * rows16, :] >= c16, one, zero)
                accs[i % n_acc] = accs[i % n_acc] + hit
            return tuple(accs)

        accs = lax.fori_loop(0, n_tiles, body, tuple(jnp.zeros((rows16, qb), BF16) for _ in range(n_acc)))
        total = (accs[0] + accs[1]) + (accs[2] + accs[3])
        return jnp.sum(total.astype(F32), axis=0, keepdims=True)

    def count_ge_hi(cand):
        return count_ge16(key_hi_s, half16(cand >> 16))

    def count_ge_lo(cand):
        return count_ge16(key_lo_s, half16((cand & 0xFFFF) - 0x8000))

    def count_where(pred):
        def body(t, part):
            hit = jnp.where(pred(key_s[t], t), 1.0, 0.0)
            return part + jnp.sum(hit.reshape(4, tk // 32, 8, qb), axis=1)
        part = lax.fori_loop(0, n_tiles, body, jnp.zeros((4, 8, qb), F32))
        return jnp.sum(jnp.sum(part, axis=0), axis=0, keepdims=True)

    def count_ge(cand):
        return count_where(lambda k, t: k >= cand)

    int_min = jnp.int32(-2 ** 31)
    topk = float(TOPK_MAX)
    theta = jnp.where(count_ge_hi(jnp.zeros((1, qb), jnp.int32)) >= topk, 0, int_min)

    def hi_step(i, theta):
        cand = theta | (jnp.int32(1) << (30 - i))
        return jnp.where(count_ge_hi(cand) >= topk, cand, theta)

    def lo_step(i, theta):
        cand = theta | (jnp.int32(1) << (15 - i))
        return jnp.where(count_ge_lo(cand) >= topk, cand, theta)

    theta = lax.fori_loop(0, 15, hi_step, theta)

    th_hi = half16(theta >> 16)

    def merge_halves(t, carry):
        for i in range(tk // rows16):
            rows = slice(i * rows16, (i + 1) * rows16)
            hi = key_hi_s[t, rows, :]
            key_lo_s[t, rows, :] = jnp.where(
                hi > th_hi, jnp.int16(0x7FFF),
                jnp.where(hi == th_hi, key_lo_s[t, rows, :], jnp.int16(-0x8000)))
        return carry

    lax.fori_loop(0, n_tiles, merge_halves, 0)
    theta = lax.fori_loop(0, 16, lo_step, theta)

    seq_bits = (seq - 1).bit_length()
    cut_s[...] = jnp.full_like(cut_s, seq)
    n_ge = count_ge(theta)

    @pl.when(jnp.max(n_ge) > topk)
    def _():
        n_gt = n_ge - count_where(lambda k, t: k == theta)
        keep = topk - n_gt

        def idx_step(i, cut):
            cand = cut | (jnp.int32(1) << (seq_bits - 1 - i))
            below = count_where(lambda k, t: (k == theta) & (t * tk + krow < cand))
            return jnp.where(below <= keep - 1.0, cand, cut)

        cut_s[...] = lax.fori_loop(0, seq_bits, idx_step, jnp.zeros((1, qb), jnp.int32))

    cut = cut_s[...]

    for h in range(B_HEADS):
        qh = q_ref[:, h * B_HEAD_DIM:(h + 1) * B_HEAD_DIM].astype(BF16)
        qlat_s[h * qb:(h + 1) * qb, :] = (
            _dot(qh, wuk_ref[h]) * ((B_HEAD_DIM ** -0.5) * LOG2E)).astype(BF16)

    m_s[...] = jnp.full_like(m_s, NEG_BIG)
    l_s[...] = jnp.zeros_like(l_s)
    acc_s[...] = jnp.zeros_like(acc_s)
    gw = DSA_HG * qb
    n_groups = B_HEADS // DSA_HG

    def attend(t, carry):
        ckv = ckv_s[pl.ds(pl.multiple_of(t * tk, tk), tk), :]
        ckvt = ckvt_s[t]
        keys = key_s[t]
        kpos = t * tk + krow
        sel = ((keys > theta) | ((keys == theta) & (kpos <= cut))) & (kpos < limit)
        bias1 = jnp.where(sel, 0.0, NEG_BIG)
        bias = bias1 if DSA_HG == 1 else jnp.concatenate([bias1] * DSA_HG, axis=1)
        lgs = [_dot_nt(ckv, qlat_s[g * gw:(g + 1) * gw, :]) for g in range(n_groups)]
        for g in range(n_groups):
            lg = lgs[g] + bias
            cols = slice(g * gw, (g + 1) * gw)
            m_old = m_s[:, cols]
            m_new = jnp.maximum(m_old, jnp.max(lg, axis=0, keepdims=True))
            alpha = jnp.exp2(m_old - m_new)
            p = jnp.exp2(lg - m_new)
            l_s[:, cols] = alpha * l_s[:, cols] + jnp.sum(p, axis=0, keepdims=True)
            acc_s[:, cols] = alpha * acc_s[:, cols] + _dot(ckvt, p.astype(BF16))
            m_s[:, cols] = m_new
        return carry

    lax.fori_loop(0, n_tiles, attend, 0)

    for h in range(B_HEADS):
        cols = slice(h * qb, (h + 1) * qb)
        o_t = (acc_s[:, cols] * (1.0 / l_s[:, cols])).astype(BF16)
        o_ref[:, h * B_HEAD_DIM:(h + 1) * B_HEAD_DIM] = _dot(wuvt_ref[h], o_t).T.astype(o_ref.dtype)


def _dsa(z3, kv_norm, ikn_pad, w_uk, w_uv):
    bsz, seq, _ = z3.shape
    w_uv = jnp.swapaxes(w_uv, 1, 2)
    qb = DSA_QB
    assert seq // BF16_SUBLANES <= 256, "bf16 partial counts of the top-k search must stay exact"
    return pl.pallas_call(
        functools.partial(_dsa_kernel, seq=seq),
        grid=(bsz, seq // qb),
        in_specs=[
            pl.BlockSpec((None, qb, B_WIDTH), lambda b, j: (b, j, COL_Q // B_WIDTH)),
            pl.BlockSpec((None, qb, B_WIDTH), lambda b, j: (b, j, COL_QI // B_WIDTH)),
            pl.BlockSpec((None, qb, LANES), lambda b, j: (b, j, COL_KW // LANES)),
            pl.BlockSpec((None, seq, KV_LATENT), lambda b, j: (b, 0, COL_CKV // KV_LATENT)),
            pl.BlockSpec((None, seq, LANES), lambda b, j: (b, 0, COL_KW // LANES)),
            pl.BlockSpec((1, KV_LATENT), lambda b, j: (0, 0)),
            pl.BlockSpec((1, LANES), lambda b, j: (0, 0)),
            pl.BlockSpec(w_uk.shape, lambda b, j: (0, 0, 0)),
            pl.BlockSpec(w_uv.shape, lambda b, j: (0, 0, 0)),
        ],
        out_specs=pl.BlockSpec((None, qb, B_WIDTH), lambda b, j: (b, j, 0)),
        out_shape=jax.ShapeDtypeStruct((bsz, seq, B_WIDTH), BF16),
        scratch_shapes=[
            pltpu.VMEM((seq, KV_LATENT), BF16),
            pltpu.VMEM((seq // DSA_TK, KV_LATENT, DSA_TK), BF16),
            pltpu.VMEM((seq, LANES), BF16),
            pltpu.VMEM((seq, LANES), BF16),
            pltpu.VMEM((seq // DSA_TK, DSA_TK, qb), jnp.int32),
            pltpu.VMEM((seq // DSA_TK, DSA_TK, qb), jnp.int16),
            pltpu.VMEM((seq // DSA_TK, DSA_TK, qb), jnp.int16),
            pltpu.VMEM((1, qb), jnp.int32),
            pltpu.VMEM((B_HEADS * qb, KV_LATENT), BF16),
            pltpu.VMEM((1, B_HEADS * qb), F32),
            pltpu.VMEM((1, B_HEADS * qb), F32),
            pltpu.VMEM((KV_LATENT, B_HEADS * qb), F32),
        ],
        compiler_params=pltpu.CompilerParams(
            dimension_semantics=("parallel", "arbitrary"), vmem_limit_bytes=VMEM_LIMIT),
        name="dsa",
    )(z3, z3, z3, z3, z3, kv_norm, ikn_pad, w_uk, w_uv)


def _pad_cols(w, width):
    return jnp.pad(w, ((0, 0), (0, width - w.shape[1])))


def _layer0_params(ab_w_in, mu, w0, w_up, a0, a_up, g_up, k_k, k_a, r_k, gn_g, gn_b, idx_k_norm):
    a_in = 3 * A_WIDTH + LORA_W + LORA_A + LORA_G
    o = 3 * A_WIDTH
    ab_w_in = ab_w_in.astype(BF16)
    w_r, w_k, w_v = ab_w_in[:, :A_WIDTH], ab_w_in[:, A_WIDTH:2 * A_WIDTH], ab_w_in[:, 2 * A_WIDTH:o]
    w_xwxa = ab_w_in[:, o:o + LORA_W + LORA_A]
    w_xg = ab_w_in[:, o + LORA_W + LORA_A:a_in]
    wb = ab_w_in[:, a_in:]
    o1 = B_WIDTH
    o2 = o1 + KV_LATENT
    o3 = o2 + IDX_HEADS * IDX_HEAD_DIM
    w_q, w_ckv, w_qi, w_kw = wb[:, :o1], wb[:, o1:o2], wb[:, o2:o3], wb[:, o3:]
    w_in = jnp.concatenate(
        [w_r, w_k, w_v, w_q, w_qi, w_ckv, _pad_cols(w_xg, 256), w_xwxa, _pad_cols(w_kw, LANES)], axis=1)
    w_in = _pad_cols(w_in, AB_IN_PAD)

    rows = [mu[:A_WIDTH], mu[A_WIDTH:2 * A_WIDTH], mu[2 * A_WIDTH:o], w0, a0, k_k, k_a,
            r_k.reshape(-1), gn_g, gn_b]
    pp = jnp.stack(rows + [jnp.zeros_like(w0)] * (RW_PP_ROWS - len(rows)), axis=0)
    mu_xx = mu[o:o + LORA_W + LORA_A][None, :]
    mu_xg = jnp.pad(mu[o + LORA_W + LORA_A:a_in], (0, 256 - LORA_G))[None, :]
    wup = jnp.pad(w_up, ((0, LANES - LORA_W), (0, 0))).astype(BF16)
    aup = jnp.pad(a_up, ((LORA_W, 0), (0, 0))).astype(BF16)
    gup = jnp.pad(g_up, ((0, 256 - LORA_G), (0, 0))).astype(BF16)
    ikn = jnp.pad(idx_k_norm, (0, LANES - IDX_HEAD_DIM))[None, :]
    return w_in, pp, mu_xx, mu_xg, wup, aup, gup, ikn


def kernel(x, norm_g, ab_w_in, ab_shift_mu, rw_w0, rw_w_up, rw_a0, rw_a_up, rw_g_up, rw_k_k, rw_k_a, rw_r_k, rw_gn_g, rw_gn_b, mla_kv_norm, mla_w_uk, mla_w_uv, idx_k_norm, ab_w_out, sg_w_in, sg_b_in, sg_ln_g, sg_ln_b, sg_w_s, sg_b_s, sg_w_out, ffn_w_up, ffn_conv_w, ffn_conv_b, ffn_w_down):
    bsz, seq, d = x.shape
    n = bsz * seq
    depth = norm_g.shape[0]
    h = x.reshape(n, d)
    ffn_w_up16 = ffn_w_up.astype(BF16)
    ffn_w_down16 = ffn_w_down.astype(BF16)
    for layer in range(depth):
        g_pre_mix, g_post_mix, g_pre_ffn, g_post_ffn = [norm_g[layer, i][None, :] for i in range(4)]
        if layer % 2 == 0:
            e = layer // 2
            w_in, pp, mu_xx, mu_xg, wup, aup, gup, ikn = _layer0_params(
                ab_w_in[e], ab_shift_mu[e], rw_w0[e], rw_w_up[e], rw_a0[e], rw_a_up[e], rw_g_up[e],
                rw_k_k[e], rw_k_a[e], rw_r_k[e], rw_gn_g[e], rw_gn_b[e], idx_k_norm[e])
            z = _norm_matmul(h, g_pre_mix, w_in, jnp.zeros((1, AB_IN_PAD), F32), gelu=False)
            z3 = z.reshape(bsz, seq, AB_IN_PAD)
            y_a = _rwkv(z3, pp, mu_xx, mu_xg, wup, aup, gup)
            y_b = _dsa(z3, mla_kv_norm[e][None, :], ikn, mla_w_uk[e].astype(BF16),
                       mla_w_uv[e].astype(BF16))
            h = _outproj(y_a.reshape(n, A_WIDTH), y_b.reshape(n, B_WIDTH),
                         ab_w_out[e].astype(BF16), h, g_post_mix)
        else:
            o = layer // 2
            z = _norm_matmul(h, g_pre_mix, sg_w_in[o].astype(BF16), sg_b_in[o][None, :], gelu=True)
            h = _sgu(z, sg_ln_g[o][None, :], sg_ln_b[o][None, :], sg_w_s[o], sg_b_s[o].T,
                     sg_w_out[o].astype(BF16), h, g_post_mix)
        h = _ffn(h, g_pre_ffn, g_post_ffn, ffn_w_up16, ffn_conv_w, ffn_conv_b[:, None, :], ffn_w_down16,
                 layer, seq=seq)
    return h.reshape(bsz, seq, d)
```
